```python
import math
import jax, jax.numpy as jnp
from jax import lax
import numpy as np

D_MODEL = 2048
BATCH = 2
SEQ = 16384
DEPTH = 4

N_MIXERS = 3
EPS = 1e-6

A_GROUPS = ((128, 1), (512, 4), (2048, 16))
A_N_GROUPS = 3
A_HEADS_PER_GROUP = 8
A_HEAD_DIM = 128
A_WIDTH = A_HEADS_PER_GROUP * A_HEAD_DIM
A_QKV_COLS = 3 * A_N_GROUPS * A_WIDTH
A_IN_COLS = A_QKV_COLS + A_WIDTH
A_ROT_DIM = A_HEAD_DIM // 4
A_ROPE_THETA = 500000.0
A_BLOCK = 128

B_WIDTH = D_MODEL
B_CONV_WIDTH = 31
B_IN_COLS = 3 * B_WIDTH

C_QK_DIM = 256
C_HEADS = D_MODEL // C_QK_DIM
C_V_DIM = 2 * C_QK_DIM
C_QK_WIDTH = C_HEADS * C_QK_DIM
C_V_WIDTH = C_HEADS * C_V_DIM
C_IN_COLS = 2 * C_QK_WIDTH + 2 * C_V_WIDTH
C_CHUNK = 128
C_ROPE_THETA = 10000.0

N_A = (DEPTH + 2) // 3
N_B = (DEPTH + 1) // 3
N_C = DEPTH // 3

kernel_name = "hybrid_dilated_conv_retention_trunk"


def rmsnorm(x, w):
    x32 = x.astype(jnp.float32)
    y = x32 * lax.rsqrt(jnp.mean(x32 * x32, axis=-1, keepdims=True) + EPS)
    return (y * w.astype(jnp.float32)).astype(x.dtype)


def rotary(t, positions, rot_dim, theta):
    inv_freq = theta ** (-jnp.arange(0, rot_dim, 2, dtype=jnp.float32) / rot_dim)
    ang = positions.astype(jnp.float32)[..., None] * inv_freq
    cos = jnp.cos(ang)[:, :, None, :]
    sin = jnp.sin(ang)[:, :, None, :]
    t32 = t.astype(jnp.float32)
    half = rot_dim // 2
    t1 = t32[..., :half]
    t2 = t32[..., half:rot_dim]
    return jnp.concatenate([t1 * cos - t2 * sin, t2 * cos + t1 * sin, t32[..., rot_dim:]], axis=-1)


def dilated_window_attention(q, k, v, window, dilation):
    b, s, h, dh = q.shape
    L = s // dilation
    w = window // dilation
    qb_len = A_BLOCK
    nb = -(-L // qb_len)
    lp = nb * qb_len

    def to_sub(t):
        t = t.reshape(b, L, dilation, h, dh).transpose(0, 2, 1, 3, 4)
        return t.reshape(b * dilation, L, h, dh)

    qs, ks, vs = to_sub(q), to_sub(k), to_sub(v)
    n = b * dilation
    qb = jnp.pad(qs, ((0, 0), (0, lp - L), (0, 0), (0, 0))).reshape(n, nb, qb_len, h, dh)
    pad_k = ((0, 0), (qb_len, lp - L), (0, 0), (0, 0))
    kb = jnp.pad(ks, pad_k).reshape(n, nb + 1, qb_len, h, dh)
    vb = jnp.pad(vs, pad_k).reshape(n, nb + 1, qb_len, h, dh)
    kblk = jnp.concatenate([kb[:, :-1], kb[:, 1:]], axis=2)
    vblk = jnp.concatenate([vb[:, :-1], vb[:, 1:]], axis=2)

    scores = jnp.einsum('nbqhd,nbkhd->nbhqk', qb, kblk) * (dh ** -0.5)
    qi = jnp.arange(qb_len)[:, None]
    kj = jnp.arange(2 * qb_len)[None, :]
    dist = qb_len + qi - kj
    blk = jnp.arange(nb)[:, None, None]
    valid = (dist >= 0) & (dist <= w) & (blk * qb_len - qb_len + kj >= 0)
    scores = jnp.where(valid[None, :, None], scores, -jnp.inf)
    m = jnp.max(scores, axis=-1, keepdims=True)
    p = jnp.exp(scores - m)
    l = jnp.sum(p, axis=-1, keepdims=True)
    o = jnp.einsum('nbhqk,nbkhd->nbqhd', p / l, vblk)
    lse = (m + jnp.log(l))[..., 0].transpose(0, 1, 3, 2)

    def from_sub(t):
        t = t.reshape((n, lp) + t.shape[3:])[:, :L]
        t = t.reshape((b, dilation, L) + t.shape[2:])
        t = jnp.moveaxis(t, 1, 2)
        return t.reshape((b, s) + t.shape[3:])

    return from_sub(o), from_sub(lse)


def mixer_a(h, positions, w_in, w_out):
    b, s, _ = h.shape
    proj = h @ w_in
    qkv = proj[..., :A_QKV_COLS].reshape(b, s, A_N_GROUPS, 3, A_HEADS_PER_GROUP, A_HEAD_DIM)
    gate = proj[..., A_QKV_COLS:]
    outs, lses = [], []
    for g, (window, dilation) in enumerate(A_GROUPS):
        q = rotary(qkv[:, :, g, 0], positions, A_ROT_DIM, A_ROPE_THETA)
        k = rotary(qkv[:, :, g, 1], positions, A_ROT_DIM, A_ROPE_THETA)
        v = qkv[:, :, g, 2].astype(jnp.float32)
        o_g, lse_g = dilated_window_attention(q, k, v, window, dilation)
        outs.append(o_g)
        lses.append(lse_g)
    alpha = jax.nn.softmax(jnp.stack(lses, axis=0), axis=0)
    o = jnp.einsum('gbsh,gbshd->bshd', alpha, jnp.stack(outs, axis=0))
    y = o.reshape(b, s, A_WIDTH).astype(h.dtype) * jax.nn.silu(gate)
    return y @ w_out


def mixer_b(h, w_in, conv_w, conv_b, ln_w, ln_b, w_out):
    proj = h @ w_in
    val, glu_gate, gate = jnp.split(proj, 3, axis=-1)
    u = val * jax.nn.sigmoid(glu_gate)
    u_pad = jnp.pad(u, ((0, 0), (B_CONV_WIDTH - 1, 0), (0, 0)))
    c = lax.conv_general_dilated(u_pad, conv_w[:, None, :].astype(u.dtype), window_strides=(1,),
                                 padding='VALID', dimension_numbers=('NWC', 'WIO', 'NWC'),
                                 feature_group_count=B_WIDTH) + conv_b
    c32 = c.astype(jnp.float32)
    mu = jnp.mean(c32, axis=-1, keepdims=True)
    var = jnp.mean(jnp.square(c32 - mu), axis=-1, keepdims=True)
    cn = ((c32 - mu) * lax.rsqrt(var + EPS) * ln_w + ln_b).astype(h.dtype)
    y = jax.nn.silu(cn) * jax.nn.silu(gate)
    return y @ w_out


def mixer_c(h, positions, w_in, w_out):
    b, s, _ = h.shape
    proj = h @ w_in
    q = proj[..., :C_QK_WIDTH].reshape(b, s, C_HEADS, C_QK_DIM)
    k = proj[..., C_QK_WIDTH:2 * C_QK_WIDTH].reshape(b, s, C_HEADS, C_QK_DIM)
    v = proj[..., 2 * C_QK_WIDTH:2 * C_QK_WIDTH + C_V_WIDTH].reshape(b, s, C_HEADS, C_V_DIM)
    gate = proj[..., 2 * C_QK_WIDTH + C_V_WIDTH:]
    q = rotary(q, positions, C_QK_DIM, C_ROPE_THETA)
    k = rotary(k, positions, C_QK_DIM, C_ROPE_THETA) * (C_QK_DIM ** -0.5)
    v = v.astype(jnp.float32)

    gammas = 1.0 - jnp.exp(jnp.linspace(math.log(1.0 / 32), math.log(1.0 / 512), C_HEADS, dtype=jnp.float32))
    log_g = jnp.log(gammas)
    idx = jnp.arange(C_CHUNK, dtype=jnp.float32)
    diff = idx[:, None] - idx[None, :]
    decay_mask = jnp.where(diff >= 0, jnp.exp(jnp.maximum(diff, 0.0)[None] * log_g[:, None, None]), 0.0)
    q_decay = jnp.exp((idx[None] + 1.0) * log_g[:, None])
    k_decay = jnp.exp((C_CHUNK - 1.0 - idx[None]) * log_g[:, None])
    chunk_decay = jnp.exp(C_CHUNK * log_g)

    n = s // C_CHUNK

    def chunks(t):
        return t.reshape(b, n, C_CHUNK, C_HEADS, t.shape[-1]).transpose(1, 0, 3, 2, 4)

    def step(state, xs):
        qc, kc, vc = xs
        inner = jnp.einsum('bhid,bhjd->bhij', qc, kc) * decay_mask
        o = (jnp.einsum('bhij,bhjv->bhiv', inner, vc)
             + jnp.einsum('bhid,bhdv->bhiv', qc, state) * q_decay[:, :, None])
        state = (state * chunk_decay[:, None, None]
                 + jnp.einsum('bhjd,bhjv->bhdv', kc * k_decay[:, :, None], vc))
        return state, o

    state0 = jnp.zeros((b, C_HEADS, C_QK_DIM, C_V_DIM), jnp.float32)
    _, o = lax.scan(step, state0, (chunks(q), chunks(k), chunks(v)))
    o = o.transpose(1, 0, 3, 2, 4).reshape(b, s, C_HEADS, C_V_DIM)
    mu = jnp.mean(o, axis=-1, keepdims=True)
    var = jnp.mean(jnp.square(o - mu), axis=-1, keepdims=True)
    o = (o - mu) * lax.rsqrt(var + EPS)
    y = o.reshape(b, s, C_V_WIDTH).astype(h.dtype) * jax.nn.silu(gate)
    return y @ w_out


def setup_inputs(seed: int = 0) -> dict:
    key = jax.random.key(seed)
    ks = jax.random.split(key, 16)
    f32 = jnp.float32

    def dense(k, shape, fan_in):
        return jax.random.normal(k, shape, f32) * (fan_in ** -0.5)

    x = jax.random.normal(ks[0], (BATCH, SEQ, D_MODEL), f32)
    offset = jax.random.randint(ks[1], (BATCH, 1), 0, 4096, dtype=jnp.int32)
    positions = (jnp.arange(SEQ, dtype=jnp.int32)[None, :] + offset).astype(jnp.int32)
    norm_w = 1.0 + 0.02 * jax.random.normal(ks[2], (DEPTH, D_MODEL), f32)
    final_norm_w = 1.0 + 0.02 * jax.random.normal(ks[3], (D_MODEL,), f32)
    a_w_in = dense(ks[4], (N_A, D_MODEL, A_IN_COLS), D_MODEL)
    a_w_out = dense(ks[5], (N_A, A_WIDTH, D_MODEL), A_WIDTH)
    b_w_in = dense(ks[6], (N_B, D_MODEL, B_IN_COLS), D_MODEL)
    b_conv_w = dense(ks[7], (N_B, B_CONV_WIDTH, B_WIDTH), B_CONV_WIDTH)
    b_conv_b = 0.01 * jax.random.normal(ks[8], (N_B, B_WIDTH), f32)
    b_ln_w = 1.0 + 0.02 * jax.random.normal(ks[9], (N_B, B_WIDTH), f32)
    b_ln_b = 0.01 * jax.random.normal(ks[10], (N_B, B_WIDTH), f32)
    b_w_out = dense(ks[11], (N_B, B_WIDTH, D_MODEL), B_WIDTH)
    c_w_in = dense(ks[12], (N_C, D_MODEL, C_IN_COLS), D_MODEL)
    c_w_out = dense(ks[13], (N_C, C_V_WIDTH, D_MODEL), C_V_WIDTH)
    return {"x": x, "positions": positions, "norm_w": norm_w, "final_norm_w": final_norm_w,
            "a_w_in": a_w_in, "a_w_out": a_w_out,
            "b_w_in": b_w_in, "b_conv_w": b_conv_w, "b_conv_b": b_conv_b,
            "b_ln_w": b_ln_w, "b_ln_b": b_ln_b, "b_w_out": b_w_out,
            "c_w_in": c_w_in, "c_w_out": c_w_out}


def reference(x, positions, norm_w, final_norm_w, a_w_in, a_w_out, b_w_in, b_conv_w, b_conv_b,
              b_ln_w, b_ln_b, b_w_out, c_w_in, c_w_out):
    for i in range(DEPTH):
        kind = i % N_MIXERS
        j = i // N_MIXERS
        h = rmsnorm(x, norm_w[i])
        if kind == 0:
            y = mixer_a(h, positions, a_w_in[j], a_w_out[j])
        elif kind == 1:
            y = mixer_b(h, b_w_in[j], b_conv_w[j], b_conv_b[j], b_ln_w[j], b_ln_b[j], b_w_out[j])
        else:
            y = mixer_c(h, positions, c_w_in[j], c_w_out[j])
        x = x + y.astype(x.dtype)
    return rmsnorm(x, final_norm_w)
```

```python
import functools
import math

import jax
import jax.numpy as jnp
from jax import lax
from jax.experimental import pallas as pl
from jax.experimental.pallas import tpu as pltpu

F32 = jnp.float32
BF16 = jnp.bfloat16

EPS = 1e-6
LANES = 128
V7X_VMEM_LIMIT = 56 * 1024 * 1024

A_GROUPS = ((128, 1), (512, 4), (2048, 16))
A_HEADS = 8
A_HEAD_DIM = 128
A_WIDTH = A_HEADS * A_HEAD_DIM
A_ROT_DIM = A_HEAD_DIM // 4
A_ROPE_THETA = 500000.0
A_BLOCK = 128
B_CONV_WIDTH = 31
B_HALO = 32
C_QK_DIM = 256
C_HEADS = 8
C_V_DIM = 512
C_CHUNK = 128
C_ROPE_THETA = 10000.0


def _params(*sem):
    return pltpu.CompilerParams(dimension_semantics=sem, vmem_limit_bytes=V7X_VMEM_LIMIT)


def _resident(shape, index_map):
    return pl.BlockSpec(shape, index_map, pipeline_mode=pl.Buffered(1))


def _rms(x, w):
    y = x * lax.rsqrt(jnp.mean(x * x, axis=-1, keepdims=True) + EPS)
    return y * w


def _silu(x):
    return x * jax.nn.sigmoid(x)


def _rmsnorm_kernel(x_ref, w_ref, h_ref):
    h_ref[...] = _rms(x_ref[...], w_ref[...]).astype(BF16)


def _rmsnorm(x, w, tm=512):
    m, d = x.shape
    return pl.pallas_call(
        _rmsnorm_kernel,
        out_shape=jax.ShapeDtypeStruct((m, d), BF16),
        grid=(m // tm,),
        in_specs=[pl.BlockSpec((tm, d), lambda i: (i, 0)), _resident((1, d), lambda i: (0, 0))],
        out_specs=pl.BlockSpec((tm, d), lambda i: (i, 0)),
        compiler_params=_params("parallel"),
        name="rmsnorm_in",
    )(x, w.reshape(1, d))


def _rope_kernel(pos_ref, fa_ref, fc_ref, acos_ref, asin_lo_ref, asin_hi_ref, ccos_ref, csin_ref):
    pos = pos_ref[...].astype(F32)
    ang_a = pos * fa_ref[...]
    sin_a = jnp.sin(ang_a)
    lane = lax.broadcasted_iota(jnp.int32, ang_a.shape, 1)
    half = A_ROT_DIM // 2
    acos_ref[...] = jnp.cos(ang_a)
    asin_lo_ref[...] = jnp.where(lane < half, -sin_a, 0.0)
    asin_hi_ref[...] = jnp.where((lane >= half) & (lane < A_ROT_DIM), sin_a, 0.0)
    ang_c = pos * fc_ref[...]
    ccos_ref[...] = jnp.cos(ang_c)
    csin_ref[...] = jnp.sin(ang_c)


def _rope_tables(positions, tm=1024):
    m = positions.size
    half_a = A_ROT_DIM // 2
    inv_a = A_ROPE_THETA ** (-jnp.arange(0, A_ROT_DIM, 2, dtype=F32) / A_ROT_DIM)
    fa = jnp.concatenate([inv_a, inv_a, jnp.zeros((LANES - 2 * half_a,), F32)]).reshape(1, LANES)
    fc = (C_ROPE_THETA ** (-jnp.arange(0, C_QK_DIM, 2, dtype=F32) / C_QK_DIM)).reshape(1, LANES)
    tab = jax.ShapeDtypeStruct((m, LANES), F32)
    row = pl.BlockSpec((tm, LANES), lambda i: (i, 0))
    return pl.pallas_call(
        _rope_kernel,
        out_shape=(tab,) * 5,
        grid=(m // tm,),
        in_specs=[pl.BlockSpec((tm, 1), lambda i: (i, 0)),
                  _resident((1, LANES), lambda i: (0, 0)), _resident((1, LANES), lambda i: (0, 0))],
        out_specs=(row,) * 5,
        compiler_params=_params("parallel"),
        name="rope_tables",
    )(positions.reshape(m, 1), fa, fc)


def _proj_a_kernel(h_ref, w_ref, cos_ref, slo_ref, shi_ref, o_ref):
    n = pl.program_id(1)
    acc = jnp.dot(h_ref[...], w_ref[...], preferred_element_type=F32)
    kind = n % 3
    rotate = (n < 9) & (kind < 2)

    @pl.when(rotate)
    def _():
        cos, slo, shi = cos_ref[...], slo_ref[...], shi_ref[...]
        scale = jnp.where(kind == 0, A_HEAD_DIM ** -0.5, 1.0).astype(F32)
        for hd in range(A_HEADS):
            sl = slice(hd * A_HEAD_DIM, (hd + 1) * A_HEAD_DIM)
            t = acc[:, sl]
            r = (t * cos + pltpu.roll(t, LANES - A_ROT_DIM // 2, 1) * slo
                 + pltpu.roll(t, A_ROT_DIM // 2, 1) * shi)
            o_ref[:, sl] = (r * scale).astype(BF16)

    @pl.when(jnp.logical_not(rotate))
    def _():
        o_ref[...] = acc.astype(BF16)


def _proj_a(h, w, tabs, tm=1024, tn=1024):
    m, d = h.shape
    n = w.shape[1]
    acos, aslo, ashi = tabs[0], tabs[1], tabs[2]
    tab = pl.BlockSpec((tm, LANES), lambda i, j: (i, 0))
    return pl.pallas_call(
        _proj_a_kernel,
        out_shape=jax.ShapeDtypeStruct((m, n), BF16),
        grid=(m // tm, n // tn),
        in_specs=[pl.BlockSpec((tm, d), lambda i, j: (i, 0)),
                  pl.BlockSpec((d, tn), lambda i, j: (0, j)), tab, tab, tab],
        out_specs=pl.BlockSpec((tm, tn), lambda i, j: (i, j)),
        compiler_params=_params("parallel", "arbitrary"),
        name="proj_a",
    )(h, w, acos, aslo, ashi)


def _proj_c_kernel(h_ref, w_ref, cos_ref, sin_ref, o_ref, *, n_rot, n_q):
    n = pl.program_id(1)
    acc = jnp.dot(h_ref[...], w_ref[...], preferred_element_type=F32)

    @pl.when(n < n_rot)
    def _():
        cos, sin = cos_ref[...], sin_ref[...]
        scale = jnp.where(n >= n_q, C_QK_DIM ** -0.5, 1.0).astype(F32)
        half = C_QK_DIM // 2
        for hd in range(acc.shape[1] // C_QK_DIM):
            lo = slice(hd * C_QK_DIM, hd * C_QK_DIM + half)
            hi = slice(hd * C_QK_DIM + half, (hd + 1) * C_QK_DIM)
            t1, t2 = acc[:, lo], acc[:, hi]
            o_ref[:, lo] = ((t1 * cos - t2 * sin) * scale).astype(BF16)
            o_ref[:, hi] = ((t2 * cos + t1 * sin) * scale).astype(BF16)

    @pl.when(n >= n_rot)
    def _():
        o_ref[...] = acc.astype(BF16)


def _proj_c(h, w, tabs, tm=1024, tn=1024):
    m, d = h.shape
    n = w.shape[1]
    qk = C_HEADS * C_QK_DIM
    tab = pl.BlockSpec((tm, LANES), lambda i, j: (i, 0))
    return pl.pallas_call(
        functools.partial(_proj_c_kernel, n_rot=2 * qk // tn, n_q=qk // tn),
        out_shape=jax.ShapeDtypeStruct((m, n), BF16),
        grid=(m // tm, n // tn),
        in_specs=[pl.BlockSpec((tm, d), lambda i, j: (i, 0)),
                  pl.BlockSpec((d, tn), lambda i, j: (0, j)), tab, tab],
        out_specs=pl.BlockSpec((tm, tn), lambda i, j: (i, j)),
        compiler_params=_params("parallel", "arbitrary"),
        name="proj_c",
    )(h, w, tabs[3], tabs[4])


def _proj_b_kernel(h_ref, wa_ref, wb_ref, u_ref, g_ref, *, n_u):
    n = pl.program_id(1)
    a = jnp.dot(h_ref[...], wa_ref[...], preferred_element_type=F32)

    @pl.when(n < n_u)
    def _():
        b = jnp.dot(h_ref[...], wb_ref[...], preferred_element_type=F32)
        u_ref[...] = (a * jax.nn.sigmoid(b)).astype(BF16)

    @pl.when(n >= n_u)
    def _():
        g_ref[...] = a.astype(BF16)


def _proj_b(h, w, tm=1024, tn=1024):
    m, d = h.shape
    width = w.shape[1] // 3
    n_u = width // tn
    wa_map = lambda i, j: (0, jnp.where(j < n_u, j, j + n_u))
    wb_map = lambda i, j: (0, n_u + jnp.minimum(j, n_u - 1))
    out = jax.ShapeDtypeStruct((m, width), BF16)
    return pl.pallas_call(
        functools.partial(_proj_b_kernel, n_u=n_u),
        out_shape=(out, out),
        grid=(m // tm, 2 * n_u),
        in_specs=[pl.BlockSpec((tm, d), lambda i, j: (i, 0)),
                  pl.BlockSpec((d, tn), wa_map), pl.BlockSpec((d, tn), wb_map)],
        out_specs=(pl.BlockSpec((tm, tn), lambda i, j: (i, jnp.minimum(j, n_u - 1))),
                   pl.BlockSpec((tm, tn), lambda i, j: (i, jnp.maximum(j - n_u, 0)))),
        compiler_params=_params("parallel", "arbitrary"),
        name="proj_b",
    )(h, w, w)


def _attn_kernel(q_ref, kc_ref, kp_ref, vc_ref, vp_ref, o_ref, lse_ref):
    blk = pl.program_id(2)
    row = lax.broadcasted_iota(jnp.int32, (A_BLOCK, A_BLOCK), 0)
    col = lax.broadcasted_iota(jnp.int32, (A_BLOCK, A_BLOCK), 1)
    keep_c = col <= row
    keep_p = (col >= row) & (blk > 0)
    nt = (((1,), (1,)), ((), ()))
    lse_ref[...] = jnp.zeros(lse_ref.shape, F32)
    for hd in range(A_HEADS):
        sl = slice(hd * A_HEAD_DIM, (hd + 1) * A_HEAD_DIM)
        q = q_ref[:, sl]
        sc = lax.dot_general(q, kc_ref[:, sl], nt, preferred_element_type=F32)
        sp = lax.dot_general(q, kp_ref[:, sl], nt, preferred_element_type=F32)
        sc = jnp.where(keep_c, sc, -jnp.inf)
        sp = jnp.where(keep_p, sp, -jnp.inf)
        mx = jnp.maximum(jnp.max(sc, axis=1, keepdims=True), jnp.max(sp, axis=1, keepdims=True))
        pc = jnp.exp(sc - mx)
        pp = jnp.exp(sp - mx)
        l = jnp.sum(pc, axis=1, keepdims=True) + jnp.sum(pp, axis=1, keepdims=True)
        o = (jnp.dot(pc.astype(BF16), vc_ref[:, sl], preferred_element_type=F32)
             + jnp.dot(pp.astype(BF16), vp_ref[:, sl], preferred_element_type=F32))
        o_ref[:, sl] = (o / l).astype(BF16)
        lse_ref[:, hd:hd + 1] = mx + jnp.log(l)


def _attention_group(proj, batch, seq, g, dilation):
    cols = proj.shape[1]
    ct = cols // A_WIDTH
    sub = seq // dilation
    nblk = sub // A_BLOCK
    view = proj.reshape(batch, sub, dilation * cols)
    base = 3 * g

    def spec(off, prev):
        if prev:
            return pl.BlockSpec((None, A_BLOCK, A_WIDTH),
                                lambda b, r, m: (b, jnp.maximum(m - 1, 0), r * ct + base + off))
        return pl.BlockSpec((None, A_BLOCK, A_WIDTH), lambda b, r, m: (b, m, r * ct + base + off))

    o, lse = pl.pallas_call(
        _attn_kernel,
        out_shape=(jax.ShapeDtypeStruct((batch, sub, dilation * A_WIDTH), BF16),
                   jax.ShapeDtypeStruct((batch, sub, dilation * LANES), F32)),
        grid=(batch, dilation, nblk),
        in_specs=[spec(0, False), spec(1, False), spec(1, True), spec(2, False), spec(2, True)],
        out_specs=(pl.BlockSpec((None, A_BLOCK, A_WIDTH), lambda b, r, m: (b, m, r)),
                   pl.BlockSpec((None, A_BLOCK, LANES), lambda b, r, m: (b, m, r))),
        compiler_params=_params("parallel", "parallel", "arbitrary"),
        name=f"attn_g{g}",
    )(view, view, view, view, view)
    return o.reshape(batch * seq, A_WIDTH), lse.reshape(batch * seq, LANES)


def _finish(y, x_ref, nw_ref, out_refs, last):
    x_new = x_ref[...] + y
    if last:
        out_refs[0][...] = _rms(x_new, nw_ref[...])
    else:
        out_refs[0][...] = x_new
        out_refs[1][...] = _rms(x_new, nw_ref[...]).astype(BF16)


def _out_shapes_specs(m, d, tm, last, index_map):
    blk = pl.BlockSpec((tm, d), index_map)
    if last:
        return (jax.ShapeDtypeStruct((m, d), F32),), (blk,)
    return (jax.ShapeDtypeStruct((m, d), F32), jax.ShapeDtypeStruct((m, d), BF16)), (blk, blk)


def _out_a_kernel(o1_ref, o2_ref, o3_ref, l1_ref, l2_ref, l3_ref, g_ref, x_ref, w_ref, nw_ref, *rest, last):
    out_refs, y_ref = rest[:-1], rest[-1]
    l1, l2, l3 = l1_ref[:, :A_HEADS], l2_ref[:, :A_HEADS], l3_ref[:, :A_HEADS]
    mx = jnp.maximum(jnp.maximum(l1, l2), l3)
    e1, e2, e3 = jnp.exp(l1 - mx), jnp.exp(l2 - mx), jnp.exp(l3 - mx)
    den = e1 + e2 + e3
    a1, a2, a3 = e1 / den, e2 / den, e3 / den
    for hd in range(A_HEADS):
        sl = slice(hd * A_HEAD_DIM, (hd + 1) * A_HEAD_DIM)
        c = slice(hd, hd + 1)
        o = (a1[:, c] * o1_ref[:, sl].astype(F32) + a2[:, c] * o2_ref[:, sl].astype(F32)
             + a3[:, c] * o3_ref[:, sl].astype(F32))
        y_ref[:, sl] = (o * _silu(g_ref[:, sl].astype(F32))).astype(BF16)
    y = jnp.dot(y_ref[...], w_ref[...], preferred_element_type=F32)
    _finish(y, x_ref, nw_ref, out_refs, last)


def _out_a(os_, lses, proj, x, w, nw, last, tm=512):
    m, d = x.shape
    gate_tile = proj.shape[1] // A_WIDTH - 1
    row = lambda i: (i, 0)
    o_spec = pl.BlockSpec((tm, A_WIDTH), row)
    l_spec = pl.BlockSpec((tm, LANES), row)
    shapes, specs = _out_shapes_specs(m, d, tm, last, row)
    return pl.pallas_call(
        functools.partial(_out_a_kernel, last=last),
        out_shape=shapes,
        grid=(m // tm,),
        in_specs=[o_spec, o_spec, o_spec, l_spec, l_spec, l_spec,
                  pl.BlockSpec((tm, A_WIDTH), lambda i: (i, gate_tile)),
                  pl.BlockSpec((tm, d), row),
                  _resident((A_WIDTH, d), lambda i: (0, 0)), _resident((1, d), lambda i: (0, 0))],
        out_specs=specs,
        scratch_shapes=[pltpu.VMEM((tm, A_WIDTH), BF16)],
        compiler_params=_params("parallel"),
        name="out_a",
    )(*os_, *lses, proj, x, w, nw.reshape(1, d))


def _out_b_kernel(u_ref, up_ref, g_ref, x_ref, cw_ref, cb_ref, lw_ref, lb_ref, w_ref, nw_ref, *rest,
                  last, tm, rc):
    out_refs, (ubuf, cbuf, y_ref) = rest[:-3], rest[-3:]
    i = pl.program_id(1)
    n_slab = ubuf.shape[0]
    halo = jnp.where(i > 0, up_ref[...].astype(F32), 0.0)
    for c in range(n_slab):
        sl = slice(c * LANES, (c + 1) * LANES)
        ubuf[c, 0:B_HALO, :] = halo[:, sl]
        ubuf[c, B_HALO:, :] = u_ref[:, sl].astype(F32)
    first = B_HALO - (B_CONV_WIDTH - 1)
    for c in range(n_slab):
        sl = slice(c * LANES, (c + 1) * LANES)
        taps = [cw_ref[j:j + 1, sl] for j in range(B_CONV_WIDTH)]
        bias = cb_ref[:, sl]

        def body(r, carry, c=c, sl=sl, taps=taps, bias=bias):
            r0 = pl.multiple_of(r * rc, rc)
            acc = jnp.broadcast_to(bias, (rc, LANES))
            for j in range(B_CONV_WIDTH):
                acc = acc + ubuf[c, pl.ds(r0 + first + j, rc), :] * taps[j]
            cbuf[pl.ds(r0, rc), sl] = acc
            return carry

        lax.fori_loop(0, tm // rc, body, 0)
    cv = cbuf[...]
    mu = jnp.mean(cv, axis=-1, keepdims=True)
    cen = cv - mu
    var = jnp.mean(cen * cen, axis=-1, keepdims=True)
    cn = cen * lax.rsqrt(var + EPS) * lw_ref[...] + lb_ref[...]
    y_ref[...] = (_silu(cn) * _silu(g_ref[...].astype(F32))).astype(BF16)
    y = jnp.dot(y_ref[...], w_ref[...], preferred_element_type=F32)
    _finish(y, x_ref, nw_ref, out_refs, last)


def _out_b(u, gate, x, conv_w, conv_b, ln_w, ln_b, w, nw, batch, seq, last, tm=256, rc=64):
    m, d = x.shape
    width = u.shape[1]
    nt = seq // tm
    hb = tm // B_HALO
    u3 = u.reshape(batch, seq, width)
    g3 = gate.reshape(batch, seq, width)
    x3 = x.reshape(batch, seq, d)
    cur = lambda b, i: (b, i, 0)
    vec = lambda a: a.reshape(1, -1)
    const = lambda b, i: (0, 0)
    shapes, specs = _out_shapes_specs(m, d, tm, last, cur)
    shapes = tuple(jax.ShapeDtypeStruct((batch, seq, d), s.dtype) for s in shapes)
    specs = tuple(pl.BlockSpec((None, tm, d), cur) for _ in specs)
    outs = pl.pallas_call(
        functools.partial(_out_b_kernel, last=last, tm=tm, rc=rc),
        out_shape=shapes,
        grid=(batch, nt),
        in_specs=[pl.BlockSpec((None, tm, width), cur),
                  pl.BlockSpec((None, B_HALO, width), lambda b, i: (b, jnp.maximum(i * hb - 1, 0), 0)),
                  pl.BlockSpec((None, tm, width), cur),
                  pl.BlockSpec((None, tm, d), cur),
                  _resident((B_CONV_WIDTH, width), const), _resident((1, width), const),
                  _resident((1, width), const), _resident((1, width), const),
                  _resident((width, d), const), _resident((1, d), const)],
        out_specs=specs,
        scratch_shapes=[pltpu.VMEM((width // LANES, tm + B_HALO, LANES), F32),
                        pltpu.VMEM((tm, width), F32), pltpu.VMEM((tm, width), BF16)],
        compiler_params=_params("parallel", "arbitrary"),
        name="out_b",
    )(u3, u3, g3, x3, conv_w, vec(conv_b), vec(ln_w), vec(ln_b), w, vec(nw))
    return tuple(o.reshape(m, d) for o in outs)


def _retention_kernel(q_ref, k_ref, v_ref, g_ref, dm_ref, qd_ref, kd_ref, cd_ref, y_ref, state, *, tc):
    @pl.when(pl.program_id(2) == 0)
    def _():
        state[...] = jnp.zeros(state.shape, F32)

    nt = (((1,), (1,)), ((), ()))
    tn = (((0,), (0,)), ((), ()))
    dm, qd, kd, cd = dm_ref[...], qd_ref[...], kd_ref[...], cd_ref[...]
    for c in range(tc // C_CHUNK):
        rs = slice(c * C_CHUNK, (c + 1) * C_CHUNK)
        q, k, v = q_ref[rs, :], k_ref[rs, :], v_ref[rs, :]
        inner = lax.dot_general(q, k, nt, preferred_element_type=F32) * dm
        st = state[...]
        cross = jnp.dot(q, st.astype(BF16), preferred_element_type=F32)
        o = jnp.dot(inner.astype(BF16), v, preferred_element_type=F32) + cross * qd
        kdec = (k.astype(F32) * kd).astype(BF16)
        state[...] = st * cd + lax.dot_general(kdec, v, tn, preferred_element_type=F32)
        mu = jnp.mean(o, axis=-1, keepdims=True)
        cen = o - mu
        var = jnp.mean(cen * cen, axis=-1, keepdims=True)
        on = cen * lax.rsqrt(var + EPS)
        y_ref[rs, :] = (on * _silu(g_ref[rs, :].astype(F32))).astype(BF16)


def _retention(proj, batch, seq, tc=512):
    cols = proj.shape[1]
    p3 = proj.reshape(batch, seq, cols)
    qk = C_HEADS * C_QK_DIM
    vw = C_HEADS * C_V_DIM
    k_off = qk // C_QK_DIM
    v_off = 2 * qk // C_V_DIM
    g_off = (2 * qk + vw) // C_V_DIM

    gammas = 1.0 - jnp.exp(jnp.linspace(math.log(1.0 / 32), math.log(1.0 / 512), C_HEADS, dtype=F32))
    log_g = jnp.log(gammas)
    idx = jnp.arange(C_CHUNK, dtype=F32)
    diff = idx[:, None] - idx[None, :]
    decay_mask = jnp.where(diff >= 0, jnp.exp(jnp.maximum(diff, 0.0)[None] * log_g[:, None, None]), 0.0)
    q_decay = jnp.exp((idx[None] + 1.0) * log_g[:, None])
    k_decay = jnp.exp((C_CHUNK - 1.0 - idx[None]) * log_g[:, None])
    chunk_decay = jnp.exp(C_CHUNK * log_g)
    qd = jnp.broadcast_to(q_decay[:, :, None], (C_HEADS, C_CHUNK, C_V_DIM))
    kd = jnp.broadcast_to(k_decay[:, :, None], (C_HEADS, C_CHUNK, C_QK_DIM))
    cd = jnp.broadcast_to(chunk_decay[:, None, None], (C_HEADS, 1, C_V_DIM))

    head = lambda b, h, i: (h, 0, 0)
    y = pl.pallas_call(
        functools.partial(_retention_kernel, tc=tc),
        out_shape=jax.ShapeDtypeStruct((batch, seq, vw), BF16),
        grid=(batch, C_HEADS, seq // tc),
        in_specs=[pl.BlockSpec((None, tc, C_QK_DIM), lambda b, h, i: (b, i, h)),
                  pl.BlockSpec((None, tc, C_QK_DIM), lambda b, h, i: (b, i, k_off + h)),
                  pl.BlockSpec((None, tc, C_V_DIM), lambda b, h, i: (b, i, v_off + h)),
                  pl.BlockSpec((None, tc, C_V_DIM), lambda b, h, i: (b, i, g_off + h)),
                  pl.BlockSpec((None, C_CHUNK, C_CHUNK), head),
                  pl.BlockSpec((None, C_CHUNK, C_V_DIM), head),
                  pl.BlockSpec((None, C_CHUNK, C_QK_DIM), head),
                  pl.BlockSpec((None, 1, C_V_DIM), head)],
        out_specs=pl.BlockSpec((None, tc, C_V_DIM), lambda b, h, i: (b, i, h)),
        scratch_shapes=[pltpu.VMEM((C_QK_DIM, C_V_DIM), F32)],
        compiler_params=_params("parallel", "parallel", "arbitrary"),
        name="retention",
    )(p3, p3, p3, p3, decay_mask, qd, kd, cd)
    return y.reshape(batch * seq, vw)


def _out_c_kernel(y_ref, x_ref, w_ref, nw_ref, *out_refs, last):
    y = jnp.dot(y_ref[...], w_ref[...], preferred_element_type=F32)
    _finish(y, x_ref, nw_ref, out_refs, last)


def _out_c(y, x, w, nw, last, tm=512):
    m, d = x.shape
    kdim = y.shape[1]
    row = lambda i: (i, 0)
    shapes, specs = _out_shapes_specs(m, d, tm, last, row)
    return pl.pallas_call(
        functools.partial(_out_c_kernel, last=last),
        out_shape=shapes,
        grid=(m // tm,),
        in_specs=[pl.BlockSpec((tm, kdim), row), pl.BlockSpec((tm, d), row),
                  _resident((kdim, d), lambda i: (0, 0)), _resident((1, d), lambda i: (0, 0))],
        out_specs=specs,
        compiler_params=_params("parallel"),
        name="out_c",
    )(y, x, w, nw.reshape(1, d))


def kernel(x, positions, norm_w, final_norm_w, a_w_in, a_w_out, b_w_in, b_conv_w, b_conv_b, b_ln_w, b_ln_b,
           b_w_out, c_w_in, c_w_out):
    batch, seq, d = x.shape
    depth = norm_w.shape[0]
    m = batch * seq
    xs = x.reshape(m, d)
    tabs = _rope_tables(positions)
    h = _rmsnorm(xs, norm_w[0])
    for i in range(depth):
        kind, j = i % 3, i // 3
        last = i == depth - 1
        nw = final_norm_w if last else norm_w[i + 1]
        if kind == 0:
            proj = _proj_a(h, a_w_in[j].astype(BF16), tabs)
            os_, lses = [], []
            for g, (_, dilation) in enumerate(A_GROUPS):
                o, lse = _attention_group(proj, batch, seq, g, dilation)
                os_.append(o)
                lses.append(lse)
            outs = _out_a(os_, lses, proj, xs, a_w_out[j].astype(BF16), nw, last)
        elif kind == 1:
            u, gate = _proj_b(h, b_w_in[j].astype(BF16))
            outs = _out_b(u, gate, xs, b_conv_w[j], b_conv_b[j], b_ln_w[j], b_ln_b[j],
                          b_w_out[j].astype(BF16), nw, batch, seq, last)
        else:
            proj = _proj_c(h, c_w_in[j].astype(BF16), tabs)
            y = _retention(proj, batch, seq)
            outs = _out_c(y, xs, c_w_out[j].astype(BF16), nw, last)
        if last:
            return outs[0].reshape(batch, seq, d)
        xs, h = outs
```

```python
import functools
import math

import jax
import jax.numpy as jnp
from jax import lax
from jax.experimental import pallas as pl
from jax.experimental.pallas import tpu as pltpu

F32 = jnp.float32
BF16 = jnp.bfloat16

EPS = 1e-6
LANES = 128
MXU_COLS = 256
V7X_VMEM_LIMIT = 56 * 1024 * 1024

A_DILATIONS = (1, 4, 16)
A_HEADS = 8
A_HEAD_DIM = 128
A_WIDTH = A_HEADS * A_HEAD_DIM
A_ROT_DIM = A_HEAD_DIM // 4
A_ROPE_THETA = 500000.0
A_BLOCK = 128
B_CONV_WIDTH = 31
B_HALO = 32
C_QK_DIM = 256
C_HEADS = 8
C_V_DIM = 512
C_CHUNK = 128
C_ROPE_THETA = 10000.0


def _params(*sem):
    return pltpu.CompilerParams(dimension_semantics=sem, vmem_limit_bytes=V7X_VMEM_LIMIT)


def _resident(shape, index_map):
    return pl.BlockSpec(shape, index_map, pipeline_mode=pl.Buffered(1))


def _rms(x, w):
    y = x * lax.rsqrt(jnp.mean(x * x, axis=-1, keepdims=True) + EPS)
    return y * w


def _silu(x):
    return x * jax.nn.sigmoid(x)


def _perm_geometry(m, tm, dil):
    tile = A_BLOCK * dil
    big = max(tm, tile)
    n_sub, n_part, rows = big // tile, big // tm, min(tm, tile) // dil
    return (m // big, n_sub, dil, n_part, rows), (None, n_sub, dil, None, rows), n_part


def _h_outputs(m, d, tm, dils, step):
    shapes = [jax.ShapeDtypeStruct((m, d), BF16)]
    specs = [pl.BlockSpec((tm, d), lambda *g: (step(*g), 0))]
    for dil in dils:
        shape, block, n_part = _perm_geometry(m, tm, dil)
        shapes.append(jax.ShapeDtypeStruct(shape + (d,), BF16))
        specs.append(pl.BlockSpec(block + (d,),
                                  lambda *g, n_part=n_part: (step(*g) // n_part, 0, 0, step(*g) % n_part, 0, 0)))
    return shapes, specs


def _write_h(h, h_refs, slab_ref, dils):
    h_refs[0][...] = h.astype(BF16)
    if not dils:
        return
    n_slab = slab_ref.shape[0]
    for c in range(n_slab):
        slab_ref[c] = h[:, c * LANES:(c + 1) * LANES]
    for ref, dil in zip(h_refs[1:], dils):
        n_sub, _, rows, _ = ref.shape
        for s in range(n_sub):
            for r in range(dil):
                for c in range(n_slab):
                    v = slab_ref[c, pl.ds(s * A_BLOCK * dil + r, rows, stride=dil), :]
                    ref[s, r, :, c * LANES:(c + 1) * LANES] = v.astype(BF16)


def _slab_scratch(d, tm, dils):
    return [pltpu.VMEM((d // LANES, tm, LANES), F32)] if dils else []


def _rmsnorm_kernel(x_ref, w_ref, *rest, dils):
    h_refs, slab = (rest[:-1], rest[-1]) if dils else (rest, None)
    _write_h(_rms(x_ref[...], w_ref[...]), h_refs, slab, dils)


def _rmsnorm(x, w, dils, tm=512):
    m, d = x.shape
    shapes, specs = _h_outputs(m, d, tm, dils, lambda i: i)
    return pl.pallas_call(
        functools.partial(_rmsnorm_kernel, dils=dils),
        out_shape=shapes,
        grid=(m // tm,),
        in_specs=[pl.BlockSpec((tm, d), lambda i: (i, 0)), _resident((1, d), lambda i: (0, 0))],
        out_specs=specs,
        scratch_shapes=_slab_scratch(d, tm, dils),
        compiler_params=_params("parallel"),
        name="rmsnorm_in",
    )(x, w.reshape(1, d))


def _to_residue_major(a, batch, seq, dil):
    c = a.shape[1]
    return a.reshape(batch, seq // (A_BLOCK * dil), A_BLOCK, dil, c).swapaxes(2, 3).reshape(batch * seq, c)


def _from_residue_major(a, batch, seq, dil):
    c = a.shape[1]
    return a.reshape(batch, seq // (A_BLOCK * dil), dil, A_BLOCK, c).swapaxes(2, 3).reshape(batch * seq, c)


def _rope_a_kernel(pos_ref, f_ref, cos_ref, sin_lo_ref, sin_hi_ref):
    ang = pos_ref[...].astype(F32) * f_ref[...]
    sin = jnp.sin(ang)
    lane = lax.broadcasted_iota(jnp.int32, ang.shape, 1)
    half = A_ROT_DIM // 2
    cos_ref[...] = jnp.cos(ang)
    sin_lo_ref[...] = jnp.where(lane < half, -sin, 0.0)
    sin_hi_ref[...] = jnp.where((lane >= half) & (lane < A_ROT_DIM), sin, 0.0)


def _rope_c_kernel(pos_ref, f_ref, cos_ref, sin_ref):
    ang = pos_ref[...].astype(F32) * f_ref[...]
    cos_ref[...] = jnp.cos(ang)
    sin_ref[...] = jnp.sin(ang)


def _rope_tables(body, n_out, pos, freqs, name, tm=1024):
    m = pos.shape[0]
    row = pl.BlockSpec((tm, LANES), lambda i: (i, 0))
    return pl.pallas_call(
        body,
        out_shape=(jax.ShapeDtypeStruct((m, LANES), F32),) * n_out,
        grid=(m // tm,),
        in_specs=[pl.BlockSpec((tm, 1), lambda i: (i, 0)), _resident((1, LANES), lambda i: (0, 0))],
        out_specs=(row,) * n_out,
        compiler_params=_params("parallel"),
        name=name,
    )(pos, freqs)


def _rope_freqs():
    half_a = A_ROT_DIM // 2
    inv_a = A_ROPE_THETA ** (-jnp.arange(0, A_ROT_DIM, 2, dtype=F32) / A_ROT_DIM)
    fa = jnp.concatenate([inv_a, inv_a, jnp.zeros((LANES - 2 * half_a,), F32)]).reshape(1, LANES)
    fc = (C_ROPE_THETA ** (-jnp.arange(0, C_QK_DIM, 2, dtype=F32) / C_QK_DIM)).reshape(1, LANES)
    return fa, fc


def _proj_a_kernel(h_ref, w_ref, cos_ref, slo_ref, shi_ref, o_ref):
    j = pl.program_id(1)
    rotate = j < 2
    scale = jnp.where(j == 0, A_HEAD_DIM ** -0.5, 1.0).astype(F32)
    cos = jnp.where(rotate, cos_ref[...] * scale, 1.0)
    slo = jnp.where(rotate, slo_ref[...] * scale, 0.0)
    shi = jnp.where(rotate, shi_ref[...] * scale, 0.0)
    h = h_ref[...]
    for c in range(o_ref.shape[1] // MXU_COLS):
        acc = jnp.dot(h, w_ref[:, c * MXU_COLS:(c + 1) * MXU_COLS], preferred_element_type=F32)
        for k in range(MXU_COLS // A_HEAD_DIM):
            t = acc[:, k * A_HEAD_DIM:(k + 1) * A_HEAD_DIM]
            r = (t * cos + pltpu.roll(t, LANES - A_ROT_DIM // 2, 1) * slo
                 + pltpu.roll(t, A_ROT_DIM // 2, 1) * shi)
            lo = c * MXU_COLS + k * A_HEAD_DIM
            o_ref[:, lo:lo + A_HEAD_DIM] = r.astype(BF16)


def _proj_a(h, w, tabs, g, tm=1024):
    m, d = h.shape
    tn = A_WIDTH
    n_tiles = 4 if g == 0 else 3
    gate_tile = w.shape[1] // tn - 1
    tab = pl.BlockSpec((tm, LANES), lambda i, j: (i, 0))
    return pl.pallas_call(
        _proj_a_kernel,
        out_shape=jax.ShapeDtypeStruct((m, n_tiles * tn), BF16),
        grid=(m // tm, n_tiles),
        in_specs=[pl.BlockSpec((tm, d), lambda i, j: (i, 0)),
                  pl.BlockSpec((d, tn), lambda i, j: (0, jnp.where(j < 3, 3 * g + j, gate_tile))),
                  tab, tab, tab],
        out_specs=pl.BlockSpec((tm, tn), lambda i, j: (i, j)),
        compiler_params=_params("parallel", "arbitrary"),
        name=f"proj_a{g}",
    )(h, w, *tabs)


def _proj_c_kernel(h_ref, w_ref, cos_ref, sin_ref, o_ref, *, n_rot, n_q):
    j = pl.program_id(1)
    rotate = j < n_rot
    scale = jnp.where(j >= n_q, C_QK_DIM ** -0.5, 1.0).astype(F32)
    cos = jnp.where(rotate, cos_ref[...] * scale, 1.0)
    sin = jnp.where(rotate, sin_ref[...] * scale, 0.0)
    h = h_ref[...]
    half = C_QK_DIM // 2
    for c in range(o_ref.shape[1] // C_QK_DIM):
        acc = jnp.dot(h, w_ref[:, c * C_QK_DIM:(c + 1) * C_QK_DIM], preferred_element_type=F32)
        t1, t2 = acc[:, :half], acc[:, half:]
        o_ref[:, c * C_QK_DIM:c * C_QK_DIM + half] = (t1 * cos - t2 * sin).astype(BF16)
        o_ref[:, c * C_QK_DIM + half:(c + 1) * C_QK_DIM] = (t2 * cos + t1 * sin).astype(BF16)


def _proj_c(h, w, tabs, tm=1024, tn=1024):
    m, d = h.shape
    n = w.shape[1]
    qk = C_HEADS * C_QK_DIM
    tab = pl.BlockSpec((tm, LANES), lambda i, j: (i, 0))
    return pl.pallas_call(
        functools.partial(_proj_c_kernel, n_rot=2 * qk // tn, n_q=qk // tn),
        out_shape=jax.ShapeDtypeStruct((m, n), BF16),
        grid=(m // tm, n // tn),
        in_specs=[pl.BlockSpec((tm, d), lambda i, j: (i, 0)),
                  pl.BlockSpec((d, tn), lambda i, j: (0, j)), tab, tab],
        out_specs=pl.BlockSpec((tm, tn), lambda i, j: (i, j)),
        compiler_params=_params("parallel", "arbitrary"),
        name="proj_c",
    )(h, w, *tabs)


def _proj_glu_kernel(h_ref, wa_ref, wb_ref, u_ref):
    h = h_ref[...]
    for c in range(u_ref.shape[1] // MXU_COLS):
        sl = slice(c * MXU_COLS, (c + 1) * MXU_COLS)
        a = jnp.dot(h, wa_ref[:, sl], preferred_element_type=F32)
        b = jnp.dot(h, wb_ref[:, sl], preferred_element_type=F32)
        u_ref[:, sl] = (a * jax.nn.sigmoid(b)).astype(BF16)


def _proj_plain_kernel(h_ref, w_ref, o_ref):
    h = h_ref[...]
    for c in range(o_ref.shape[1] // MXU_COLS):
        sl = slice(c * MXU_COLS, (c + 1) * MXU_COLS)
        o_ref[:, sl] = jnp.dot(h, w_ref[:, sl], preferred_element_type=F32).astype(BF16)


def _proj_b(h, w, tm=1024, tn=1024):
    m, d = h.shape
    width = w.shape[1] // 3
    nt = width // tn
    h_spec = pl.BlockSpec((tm, d), lambda i, j: (i, 0))
    o_spec = pl.BlockSpec((tm, tn), lambda i, j: (i, j))
    out = jax.ShapeDtypeStruct((m, width), BF16)
    u = pl.pallas_call(
        _proj_glu_kernel, out_shape=out, grid=(m // tm, nt),
        in_specs=[h_spec, pl.BlockSpec((d, tn), lambda i, j: (0, j)),
                  pl.BlockSpec((d, tn), lambda i, j: (0, nt + j))],
        out_specs=o_spec, compiler_params=_params("parallel", "arbitrary"), name="proj_b_glu",
    )(h, w, w)
    gate = pl.pallas_call(
        _proj_plain_kernel, out_shape=out, grid=(m // tm, nt),
        in_specs=[h_spec, pl.BlockSpec((d, tn), lambda i, j: (0, 2 * nt + j))],
        out_specs=o_spec, compiler_params=_params("parallel", "arbitrary"), name="proj_b_gate",
    )(h, w)
    return u, gate


def _attn_kernel(q_ref, kc_ref, kp_ref, vc_ref, vp_ref, o_ref, lse_ref):
    blk = pl.program_id(2)
    row = lax.broadcasted_iota(jnp.int32, (A_BLOCK, A_BLOCK), 0)
    col = lax.broadcasted_iota(jnp.int32, (A_BLOCK, A_BLOCK), 1)
    keep_c = col <= row
    keep_p = (col >= row) & (blk > 0)
    nt = (((1,), (1,)), ((), ()))
    lse_ref[...] = jnp.zeros(lse_ref.shape, F32)
    for hd in range(A_HEADS):
        sl = slice(hd * A_HEAD_DIM, (hd + 1) * A_HEAD_DIM)
        q = q_ref[:, sl]
        sc = lax.dot_general(q, kc_ref[:, sl], nt, preferred_element_type=F32)
        sp = lax.dot_general(q, kp_ref[:, sl], nt, preferred_element_type=F32)
        sc = jnp.where(keep_c, sc, -jnp.inf)
        sp = jnp.where(keep_p, sp, -jnp.inf)
        mx = jnp.maximum(jnp.max(sc, axis=1, keepdims=True), jnp.max(sp, axis=1, keepdims=True))
        pc = jnp.exp(sc - mx)
        pp = jnp.exp(sp - mx)
        l = jnp.sum(pc, axis=1, keepdims=True) + jnp.sum(pp, axis=1, keepdims=True)
        o = (jnp.dot(pc.astype(BF16), vc_ref[:, sl], preferred_element_type=F32)
             + jnp.dot(pp.astype(BF16), vp_ref[:, sl], preferred_element_type=F32))
        o_ref[:, sl] = (o / l).astype(BF16)
        lse_ref[:, hd:hd + 1] = mx + jnp.log(l)


def _attention_group(proj, batch, seq, dil):
    m = batch * seq
    per_batch = seq // A_BLOCK
    nblk = seq // (A_BLOCK * dil)

    def spec(tile, prev):
        if prev:
            return pl.BlockSpec((A_BLOCK, A_WIDTH),
                                lambda b, r, t: (b * per_batch + jnp.maximum(t - 1, 0) * dil + r, tile))
        return pl.BlockSpec((A_BLOCK, A_WIDTH), lambda b, r, t: (b * per_batch + t * dil + r, tile))

    return pl.pallas_call(
        _attn_kernel,
        out_shape=(jax.ShapeDtypeStruct((m, A_WIDTH), BF16), jax.ShapeDtypeStruct((m, LANES), F32)),
        grid=(batch, dil, nblk),
        in_specs=[spec(0, False), spec(1, False), spec(1, True), spec(2, False), spec(2, True)],
        out_specs=(pl.BlockSpec((A_BLOCK, A_WIDTH), lambda b, r, t: (b * per_batch + t * dil + r, 0)),
                   pl.BlockSpec((A_BLOCK, LANES), lambda b, r, t: (b * per_batch + t * dil + r, 0))),
        compiler_params=_params("parallel", "parallel", "arbitrary"),
        name=f"attn_d{dil}",
    )(proj, proj, proj, proj, proj)


def _finish(y, x_ref, nw_ref, out_refs, slab_ref, last, dils):
    x_new = x_ref[...] + y
    h = _rms(x_new, nw_ref[...])
    if last:
        out_refs[0][...] = h
    else:
        out_refs[0][...] = x_new
        _write_h(h, out_refs[1:], slab_ref, dils)


def _tail_outputs(m, d, tm, last, dils, step):
    x_shape = jax.ShapeDtypeStruct((m, d), F32)
    x_spec = pl.BlockSpec((tm, d), lambda *g: (step(*g), 0))
    if last:
        return [x_shape], [x_spec]
    shapes, specs = _h_outputs(m, d, tm, dils, step)
    return [x_shape] + shapes, [x_spec] + specs


def _split_tail(rest, n_scratch, last, dils):
    n_out = 1 if last else 2 + len(dils)
    outs = rest[:n_out]
    slab = rest[n_out] if (dils and not last) else None
    return outs, slab, rest[len(rest) - n_scratch:]


def _out_a_kernel(o1_ref, o2_ref, o3_ref, l1_ref, l2_ref, l3_ref, g_ref, x_ref, w_ref, nw_ref, *rest, last, dils):
    out_refs, slab, (y_ref,) = _split_tail(rest, 1, last, dils)
    l1, l2, l3 = l1_ref[:, :A_HEADS], l2_ref[:, :A_HEADS], l3_ref[:, :A_HEADS]
    mx = jnp.maximum(jnp.maximum(l1, l2), l3)
    e1, e2, e3 = jnp.exp(l1 - mx), jnp.exp(l2 - mx), jnp.exp(l3 - mx)
    den = e1 + e2 + e3
    a1, a2, a3 = e1 / den, e2 / den, e3 / den
    for hd in range(A_HEADS):
        sl = slice(hd * A_HEAD_DIM, (hd + 1) * A_HEAD_DIM)
        c = slice(hd, hd + 1)
        o = (a1[:, c] * o1_ref[:, sl].astype(F32) + a2[:, c] * o2_ref[:, sl].astype(F32)
             + a3[:, c] * o3_ref[:, sl].astype(F32))
        y_ref[:, sl] = (o * _silu(g_ref[:, sl].astype(F32))).astype(BF16)
    y = jnp.dot(y_ref[...], w_ref[...], preferred_element_type=F32)
    _finish(y, x_ref, nw_ref, out_refs, slab, last, dils)


def _out_a(os_, lses, proj0, x, w, nw, last, dils, tm=512):
    m, d = x.shape
    row = lambda i: (i, 0)
    o_spec = pl.BlockSpec((tm, A_WIDTH), row)
    l_spec = pl.BlockSpec((tm, LANES), row)
    shapes, specs = _tail_outputs(m, d, tm, last, dils, lambda i: i)
    return pl.pallas_call(
        functools.partial(_out_a_kernel, last=last, dils=dils),
        out_shape=shapes,
        grid=(m // tm,),
        in_specs=[o_spec, o_spec, o_spec, l_spec, l_spec, l_spec,
                  pl.BlockSpec((tm, A_WIDTH), lambda i: (i, 3)),
                  pl.BlockSpec((tm, d), row),
                  _resident((A_WIDTH, d), lambda i: (0, 0)), _resident((1, d), lambda i: (0, 0))],
        out_specs=specs,
        scratch_shapes=([] if last else _slab_scratch(d, tm, dils)) + [pltpu.VMEM((tm, A_WIDTH), BF16)],
        compiler_params=_params("parallel"),
        name="out_a",
    )(*os_, *lses, proj0, x, w, nw.reshape(1, d))


def _out_b_kernel(u_ref, up_ref, g_ref, x_ref, cw_ref, cb_ref, lw_ref, lb_ref, w_ref, nw_ref, *rest,
                  last, dils, tm, rc):
    out_refs, slab, (ubuf, cbuf, y_ref) = _split_tail(rest, 3, last, dils)
    i = pl.program_id(1)
    n_slab = ubuf.shape[0]
    halo = jnp.where(i > 0, up_ref[...].astype(F32), 0.0)
    for c in range(n_slab):
        sl = slice(c * LANES, (c + 1) * LANES)
        ubuf[c, 0:B_HALO, :] = halo[:, sl]
        ubuf[c, B_HALO:, :] = u_ref[:, sl].astype(F32)
    first = B_HALO - (B_CONV_WIDTH - 1)
    for c in range(n_slab):
        sl = slice(c * LANES, (c + 1) * LANES)
        taps = [cw_ref[j:j + 1, sl] for j in range(B_CONV_WIDTH)]
        bias = cb_ref[:, sl]

        def body(r, carry, c=c, sl=sl, taps=taps, bias=bias):
            r0 = pl.multiple_of(r * rc, rc)
            acc = jnp.broadcast_to(bias, (rc, LANES))
            for j in range(B_CONV_WIDTH):
                acc = acc + ubuf[c, pl.ds(r0 + first + j, rc), :] * taps[j]
            cbuf[pl.ds(r0, rc), sl] = acc
            return carry

        lax.fori_loop(0, tm // rc, body, 0)
    cv = cbuf[...]
    mu = jnp.mean(cv, axis=-1, keepdims=True)
    cen = cv - mu
    var = jnp.mean(cen * cen, axis=-1, keepdims=True)
    cn = cen * lax.rsqrt(var + EPS) * lw_ref[...] + lb_ref[...]
    y_ref[...] = (_silu(cn) * _silu(g_ref[...].astype(F32))).astype(BF16)
    y = jnp.dot(y_ref[...], w_ref[...], preferred_element_type=F32)
    _finish(y, x_ref, nw_ref, out_refs, slab, last, dils)


def _out_b(u, gate, x, conv_w, conv_b, ln_w, ln_b, w, nw, batch, seq, last, dils, tm=256, rc=64):
    m, d = x.shape
    width = u.shape[1]
    nt = seq // tm
    hb = tm // B_HALO
    cur = lambda b, i: (b * nt + i, 0)
    vec = lambda a: a.reshape(1, -1)
    const = lambda b, i: (0, 0)
    shapes, specs = _tail_outputs(m, d, tm, last, dils, lambda b, i: b * nt + i)
    return pl.pallas_call(
        functools.partial(_out_b_kernel, last=last, dils=dils, tm=tm, rc=rc),
        out_shape=shapes,
        grid=(batch, nt),
        in_specs=[pl.BlockSpec((tm, width), cur),
                  pl.BlockSpec((B_HALO, width), lambda b, i: (jnp.maximum((b * nt + i) * hb - 1, 0), 0)),
                  pl.BlockSpec((tm, width), cur),
                  pl.BlockSpec((tm, d), cur),
                  _resident((B_CONV_WIDTH, width), const), _resident((1, width), const),
                  _resident((1, width), const), _resident((1, width), const),
                  _resident((width, d), const), _resident((1, d), const)],
        out_specs=specs,
        scratch_shapes=([] if last else _slab_scratch(d, tm, dils))
        + [pltpu.VMEM((width // LANES, tm + B_HALO, LANES), F32),
           pltpu.VMEM((tm, width), F32), pltpu.VMEM((tm, width), BF16)],
        compiler_params=_params("parallel", "arbitrary"),
        name="out_b",
    )(u, u, gate, x, conv_w, vec(conv_b), vec(ln_w), vec(ln_b), w, vec(nw))


def _retention_kernel(q_ref, k_ref, v_ref, g_ref, dm_ref, qd_ref, kd_ref, cd_ref, y_ref, state, *, tc):
    @pl.when(pl.program_id(2) == 0)
    def _():
        state[...] = jnp.zeros(state.shape, F32)

    nt = (((1,), (1,)), ((), ()))
    tn = (((0,), (0,)), ((), ()))
    dm, qd, kd, cd = dm_ref[...], qd_ref[...], kd_ref[...], cd_ref[...]
    for c in range(tc // C_CHUNK):
        rs = slice(c * C_CHUNK, (c + 1) * C_CHUNK)
        q, k, v = q_ref[rs, :], k_ref[rs, :], v_ref[rs, :]
        inner = lax.dot_general(q, k, nt, preferred_element_type=F32) * dm
        st = state[...]
        cross = jnp.dot(q, st.astype(BF16), preferred_element_type=F32)
        o = jnp.dot(inner.astype(BF16), v, preferred_element_type=F32) + cross * qd
        kdec = (k.astype(F32) * kd).astype(BF16)
        state[...] = st * cd + lax.dot_general(kdec, v, tn, preferred_element_type=F32)
        mu = jnp.mean(o, axis=-1, keepdims=True)
        cen = o - mu
        var = jnp.mean(cen * cen, axis=-1, keepdims=True)
        on = cen * lax.rsqrt(var + EPS)
        y_ref[rs, :] = (on * _silu(g_ref[rs, :].astype(F32))).astype(BF16)


def _retention(proj, batch, seq, tc=512):
    m = batch * seq
    nt = seq // tc
    qk = C_HEADS * C_QK_DIM
    vw = C_HEADS * C_V_DIM
    k_off = qk // C_QK_DIM
    v_off = 2 * qk // C_V_DIM
    g_off = (2 * qk + vw) // C_V_DIM

    gammas = 1.0 - jnp.exp(jnp.linspace(math.log(1.0 / 32), math.log(1.0 / 512), C_HEADS, dtype=F32))
    log_g = jnp.log(gammas)
    idx = jnp.arange(C_CHUNK, dtype=F32)
    diff = idx[:, None] - idx[None, :]
    decay_mask = jnp.where(diff >= 0, jnp.exp(jnp.maximum(diff, 0.0)[None] * log_g[:, None, None]), 0.0)
    q_decay = jnp.exp((idx[None] + 1.0) * log_g[:, None])
    k_decay = jnp.exp((C_CHUNK - 1.0 - idx[None]) * log_g[:, None])
    chunk_decay = jnp.exp(C_CHUNK * log_g)
    qd = jnp.broadcast_to(q_decay[:, :, None], (C_HEADS, C_CHUNK, C_V_DIM))
    kd = jnp.broadcast_to(k_decay[:, :, None], (C_HEADS, C_CHUNK, C_QK_DIM))
    cd = jnp.broadcast_to(chunk_decay[:, None, None], (C_HEADS, 1, C_V_DIM))

    head = lambda b, h, i: (h, 0, 0)
    rows = lambda off: (lambda b, h, i: (b * nt + i, off + h))
    return pl.pallas_call(
        functools.partial(_retention_kernel, tc=tc),
        out_shape=jax.ShapeDtypeStruct((m, vw), BF16),
        grid=(batch, C_HEADS, nt),
        in_specs=[pl.BlockSpec((tc, C_QK_DIM), rows(0)),
                  pl.BlockSpec((tc, C_QK_DIM), rows(k_off)),
                  pl.BlockSpec((tc, C_V_DIM), rows(v_off)),
                  pl.BlockSpec((tc, C_V_DIM), rows(g_off)),
                  pl.BlockSpec((None, C_CHUNK, C_CHUNK), head),
                  pl.BlockSpec((None, C_CHUNK, C_V_DIM), head),
                  pl.BlockSpec((None, C_CHUNK, C_QK_DIM), head),
                  pl.BlockSpec((None, 1, C_V_DIM), head)],
        out_specs=pl.BlockSpec((tc, C_V_DIM), rows(0)),
        scratch_shapes=[pltpu.VMEM((C_QK_DIM, C_V_DIM), F32)],
        compiler_params=_params("parallel", "parallel", "arbitrary"),
        name="retention",
    )(proj, proj, proj, proj, decay_mask, qd, kd, cd)


def _out_c_kernel(y_ref, x_ref, w_ref, nw_ref, *rest, last, dils):
    out_refs, slab, _ = _split_tail(rest, 0, last, dils)
    y = jnp.dot(y_ref[...], w_ref[...], preferred_element_type=F32)
    _finish(y, x_ref, nw_ref, out_refs, slab, last, dils)


def _out_c(y, x, w, nw, last, dils, tm=256):
    m, d = x.shape
    kdim = y.shape[1]
    row = lambda i: (i, 0)
    shapes, specs = _tail_outputs(m, d, tm, last, dils, lambda i: i)
    return pl.pallas_call(
        functools.partial(_out_c_kernel, last=last, dils=dils),
        out_shape=shapes,
        grid=(m // tm,),
        in_specs=[pl.BlockSpec((tm, kdim), row), pl.BlockSpec((tm, d), row),
                  _resident((kdim, d), lambda i: (0, 0)), _resident((1, d), lambda i: (0, 0))],
        out_specs=specs,
        scratch_shapes=[] if last else _slab_scratch(d, tm, dils),
        compiler_params=_params("parallel"),
        name="out_c",
    )(y, x, w, nw.reshape(1, d))


def kernel(x, positions, norm_w, final_norm_w, a_w_in, a_w_out, b_w_in, b_conv_w, b_conv_b, b_ln_w, b_ln_b,
           b_w_out, c_w_in, c_w_out):
    batch, seq, d = x.shape
    depth = norm_w.shape[0]
    m = batch * seq
    xs = x.reshape(m, d)
    perm_dils = tuple(dl for dl in A_DILATIONS if dl > 1)

    def next_dils(i):
        return perm_dils if (i + 1 < depth and (i + 1) % 3 == 0) else ()

    fa, fc = _rope_freqs()
    pos = positions.reshape(m, 1)
    tabs_a = [_rope_tables(_rope_a_kernel, 3, pos if dl == 1 else _to_residue_major(pos, batch, seq, dl), fa,
                           f"rope_a{dl}") for dl in A_DILATIONS]
    tabs_c = _rope_tables(_rope_c_kernel, 2, pos, fc, "rope_c")

    hs = _rmsnorm(xs, norm_w[0], perm_dils)
    for i in range(depth):
        kind, j = i % 3, i // 3
        last = i == depth - 1
        nw = final_norm_w if last else norm_w[i + 1]
        dils = next_dils(i)
        if kind == 0:
            w_in = a_w_in[j].astype(BF16)
            os_, lses, proj0 = [], [], None
            for g, dl in enumerate(A_DILATIONS):
                h_g = hs[g].reshape(m, d)
                proj = _proj_a(h_g, w_in, tabs_a[g], g)
                proj0 = proj if g == 0 else proj0
                o, lse = _attention_group(proj, batch, seq, dl)
                if dl > 1:
                    o, lse = _from_residue_major(o, batch, seq, dl), _from_residue_major(lse, batch, seq, dl)
                os_.append(o)
                lses.append(lse)
            outs = _out_a(os_, lses, proj0, xs, a_w_out[j].astype(BF16), nw, last, dils)
        elif kind == 1:
            u, gate = _proj_b(hs[0], b_w_in[j].astype(BF16))
            outs = _out_b(u, gate, xs, b_conv_w[j], b_conv_b[j], b_ln_w[j], b_ln_b[j],
                          b_w_out[j].astype(BF16), nw, batch, seq, last, dils)
        else:
            proj = _proj_c(hs[0], c_w_in[j].astype(BF16), tabs_c)
            y = _retention(proj, batch, seq)
            outs = _out_c(y, xs, c_w_out[j].astype(BF16), nw, last, dils)
        if last:
            return outs[0].reshape(batch, seq, d)
        xs, hs = outs[0], outs[1:]
```

```python
import functools
import math

import jax
import jax.numpy as jnp
from jax import lax
from jax.experimental import pallas as pl
from jax.experimental.pallas import tpu as pltpu

F32 = jnp.float32
BF16 = jnp.bfloat16

EPS = 1e-6
LANES = 128
MXU_COLS = 256
V7X_VMEM_LIMIT = 56 * 1024 * 1024

A_DILATIONS = (1, 4, 16)
A_HEADS = 8
A_HEAD_DIM = 128
A_WIDTH = A_HEADS * A_HEAD_DIM
A_ROT_DIM = A_HEAD_DIM // 4
A_ROPE_THETA = 500000.0
A_BLOCK = 128
B_CONV_WIDTH = 31
B_HALO = 32
C_QK_DIM = 256
C_HEADS = 8
C_V_DIM = 512
C_CHUNK = 128
C_ROPE_THETA = 10000.0


def _params(*sem):
    return pltpu.CompilerParams(dimension_semantics=sem, vmem_limit_bytes=V7X_VMEM_LIMIT)


def _resident(shape, index_map):
    return pl.BlockSpec(shape, index_map, pipeline_mode=pl.Buffered(1))


def _rms(x, w):
    y = x * lax.rsqrt(jnp.mean(x * x, axis=-1, keepdims=True) + EPS)
    return y * w


def _silu(x):
    return x * jax.nn.sigmoid(x)


def _perm_geometry(m, tm, dil):
    tile = A_BLOCK * dil
    big = max(tm, tile)
    n_sub, n_part, rows = big // tile, big // tm, min(tm, tile) // dil
    return (m // big, n_sub, dil, n_part, rows), (None, n_sub, dil, None, rows), n_part


def _h_outputs(m, d, tm, dils, step):
    shapes = [jax.ShapeDtypeStruct((m, d), BF16)]
    specs = [pl.BlockSpec((tm, d), lambda *g: (step(*g), 0))]
    for dil in dils:
        shape, block, n_part = _perm_geometry(m, tm, dil)
        shapes.append(jax.ShapeDtypeStruct(shape + (d,), BF16))
        specs.append(pl.BlockSpec(block + (d,),
                                  lambda *g, n_part=n_part: (step(*g) // n_part, 0, 0, step(*g) % n_part, 0, 0)))
    return shapes, specs


def _write_h(h, h_refs, slab_ref, dils):
    h_refs[0][...] = h.astype(BF16)
    if not dils:
        return
    n_slab = slab_ref.shape[0]
    for c in range(n_slab):
        slab_ref[c] = h[:, c * LANES:(c + 1) * LANES]
    for ref, dil in zip(h_refs[1:], dils):
        n_sub, _, rows, _ = ref.shape
        for s in range(n_sub):
            for r in range(dil):
                for c in range(n_slab):
                    v = slab_ref[c, pl.ds(s * A_BLOCK * dil + r, rows, stride=dil), :]
                    ref[s, r, :, c * LANES:(c + 1) * LANES] = v.astype(BF16)


def _slab_scratch(d, tm, dils):
    return [pltpu.VMEM((d // LANES, tm, LANES), F32)] if dils else []


def _rmsnorm_kernel(x_ref, w_ref, *rest, dils):
    h_refs, slab = (rest[:-1], rest[-1]) if dils else (rest, None)
    _write_h(_rms(x_ref[...], w_ref[...]), h_refs, slab, dils)


def _rmsnorm(x, w, dils, tm=512):
    m, d = x.shape
    shapes, specs = _h_outputs(m, d, tm, dils, lambda i: i)
    return pl.pallas_call(
        functools.partial(_rmsnorm_kernel, dils=dils),
        out_shape=shapes,
        grid=(m // tm,),
        in_specs=[pl.BlockSpec((tm, d), lambda i: (i, 0)), _resident((1, d), lambda i: (0, 0))],
        out_specs=specs,
        scratch_shapes=_slab_scratch(d, tm, dils),
        compiler_params=_params("parallel"),
        name="rmsnorm_in",
    )(x, w.reshape(1, d))


def _to_residue_major(a, batch, seq, dil):
    c = a.shape[1]
    return a.reshape(batch, seq // (A_BLOCK * dil), A_BLOCK, dil, c).swapaxes(2, 3).reshape(batch * seq, c)


def _rope_a_kernel(pos_ref, f_ref, cos_ref, sin_lo_ref, sin_hi_ref):
    ang = pos_ref[...].astype(F32) * f_ref[...]
    sin = jnp.sin(ang)
    lane = lax.broadcasted_iota(jnp.int32, ang.shape, 1)
    half = A_ROT_DIM // 2
    cos_ref[...] = jnp.cos(ang)
    sin_lo_ref[...] = jnp.where(lane < half, -sin, 0.0)
    sin_hi_ref[...] = jnp.where((lane >= half) & (lane < A_ROT_DIM), sin, 0.0)


def _rope_c_kernel(pos_ref, f_ref, cos_ref, sin_ref):
    ang = pos_ref[...].astype(F32) * f_ref[...]
    cos_ref[...] = jnp.cos(ang)
    sin_ref[...] = jnp.sin(ang)


def _rope_tables(body, n_out, pos, freqs, name, tm=1024):
    m = pos.shape[0]
    row = pl.BlockSpec((tm, LANES), lambda i: (i, 0))
    return pl.pallas_call(
        body,
        out_shape=(jax.ShapeDtypeStruct((m, LANES), F32),) * n_out,
        grid=(m // tm,),
        in_specs=[pl.BlockSpec((tm, 1), lambda i: (i, 0)), _resident((1, LANES), lambda i: (0, 0))],
        out_specs=(row,) * n_out,
        compiler_params=_params("parallel"),
        name=name,
    )(pos, freqs)


def _rope_freqs():
    half_a = A_ROT_DIM // 2
    inv_a = A_ROPE_THETA ** (-jnp.arange(0, A_ROT_DIM, 2, dtype=F32) / A_ROT_DIM)
    fa = jnp.concatenate([inv_a, inv_a, jnp.zeros((LANES - 2 * half_a,), F32)]).reshape(1, LANES)
    fc = (C_ROPE_THETA ** (-jnp.arange(0, C_QK_DIM, 2, dtype=F32) / C_QK_DIM)).reshape(1, LANES)
    return fa, fc


def _rotary_a(t, cos, slo, shi):
    return (t * cos + pltpu.roll(t, LANES - A_ROT_DIM // 2, 1) * slo + pltpu.roll(t, A_ROT_DIM // 2, 1) * shi)


def _proj_a_kernel(h_ref, wqkv_ref, *rest, has_gate):
    if has_gate:
        wg_ref, cos_ref, slo_ref, shi_ref, o_ref = rest
    else:
        wg_ref, (cos_ref, slo_ref, shi_ref, o_ref) = None, rest
    h = h_ref[...]
    cos, slo, shi = cos_ref[...], slo_ref[...], shi_ref[...]
    scale = A_HEAD_DIM ** -0.5
    q_tabs = (cos * scale, slo * scale, shi * scale)
    for c in range(o_ref.shape[1] // MXU_COLS):
        tile, off = divmod(c * MXU_COLS, A_WIDTH)
        w_chunk = (wg_ref[:, off:off + MXU_COLS] if tile == 3
                   else wqkv_ref[:, c * MXU_COLS:(c + 1) * MXU_COLS])
        acc = jnp.dot(h, w_chunk, preferred_element_type=F32)
        for k in range(MXU_COLS // A_HEAD_DIM):
            t = acc[:, k * A_HEAD_DIM:(k + 1) * A_HEAD_DIM]
            if tile == 0:
                t = _rotary_a(t, *q_tabs)
            elif tile == 1:
                t = _rotary_a(t, cos, slo, shi)
            elif tile == 3:
                t = _silu(t)
            lo = c * MXU_COLS + k * A_HEAD_DIM
            o_ref[:, lo:lo + A_HEAD_DIM] = t.astype(BF16)


def _proj_a(h, w, tabs, g, tm=1024):
    m, d = h.shape
    has_gate = g == 0
    n_out = (4 if has_gate else 3) * A_WIDTH
    gate_tile = w.shape[1] // A_WIDTH - 1
    tab = pl.BlockSpec((tm, LANES), lambda i: (i, 0))
    w_specs = [_resident((d, 3 * A_WIDTH), lambda i: (0, g))]
    w_args = [w]
    if has_gate:
        w_specs.append(_resident((d, A_WIDTH), lambda i: (0, gate_tile)))
        w_args.append(w)
    return pl.pallas_call(
        functools.partial(_proj_a_kernel, has_gate=has_gate),
        out_shape=jax.ShapeDtypeStruct((m, n_out), BF16),
        grid=(m // tm,),
        in_specs=[pl.BlockSpec((tm, d), lambda i: (i, 0))] + w_specs + [tab, tab, tab],
        out_specs=pl.BlockSpec((tm, n_out), lambda i: (i, 0)),
        compiler_params=_params("parallel"),
        name=f"proj_a{g}",
    )(h, *w_args, *tabs)


def _proj_c_qk_kernel(h_ref, w_ref, cos_ref, sin_ref, o_ref, *, n_q):
    j = pl.program_id(1)
    scale = jnp.where(j >= n_q, C_QK_DIM ** -0.5, 1.0).astype(F32)
    cos = cos_ref[...] * scale
    sin = sin_ref[...] * scale
    h = h_ref[...]
    half = C_QK_DIM // 2
    for c in range(o_ref.shape[1] // C_QK_DIM):
        acc = jnp.dot(h, w_ref[:, c * C_QK_DIM:(c + 1) * C_QK_DIM], preferred_element_type=F32)
        t1, t2 = acc[:, :half], acc[:, half:]
        o_ref[:, c * C_QK_DIM:c * C_QK_DIM + half] = (t1 * cos - t2 * sin).astype(BF16)
        o_ref[:, c * C_QK_DIM + half:(c + 1) * C_QK_DIM] = (t2 * cos + t1 * sin).astype(BF16)


def _proj_glu_kernel(h_ref, wa_ref, wb_ref, u_ref):
    h = h_ref[...]
    for c in range(u_ref.shape[1] // MXU_COLS):
        sl = slice(c * MXU_COLS, (c + 1) * MXU_COLS)
        a = jnp.dot(h, wa_ref[:, sl], preferred_element_type=F32)
        b = jnp.dot(h, wb_ref[:, sl], preferred_element_type=F32)
        u_ref[:, sl] = (a * jax.nn.sigmoid(b)).astype(BF16)


def _proj_act_kernel(h_ref, w_ref, o_ref, *, act):
    h = h_ref[...]
    for c in range(o_ref.shape[1] // MXU_COLS):
        sl = slice(c * MXU_COLS, (c + 1) * MXU_COLS)
        acc = jnp.dot(h, w_ref[:, sl], preferred_element_type=F32)
        o_ref[:, sl] = (_silu(acc) if act else acc).astype(BF16)


def _proj_cols(body, h, ws, first_tiles, n_tiles, extra=(), name="proj", tm=1024, tn=1024):
    m, d = h.shape
    tab = pl.BlockSpec((tm, LANES), lambda i, j: (i, 0))
    w_specs = [pl.BlockSpec((d, tn), lambda i, j, f=f: (0, f + j)) for f in first_tiles]
    return pl.pallas_call(
        body,
        out_shape=jax.ShapeDtypeStruct((m, n_tiles * tn), BF16),
        grid=(m // tm, n_tiles),
        in_specs=[pl.BlockSpec((tm, d), lambda i, j: (i, 0))] + w_specs + [tab] * len(extra),
        out_specs=pl.BlockSpec((tm, tn), lambda i, j: (i, j)),
        compiler_params=_params("parallel", "arbitrary"),
        name=name,
    )(h, *ws, *extra)


def _proj_c(h, w, tabs, tn=1024):
    qk_tiles = 2 * C_HEADS * C_QK_DIM // tn
    v_tiles = C_HEADS * C_V_DIM // tn
    qk = _proj_cols(functools.partial(_proj_c_qk_kernel, n_q=qk_tiles // 2), h, [w], [0], qk_tiles, tabs,
                    "proj_c_qk")
    v = _proj_cols(functools.partial(_proj_act_kernel, act=False), h, [w], [qk_tiles], v_tiles, name="proj_c_v")
    gate = _proj_cols(functools.partial(_proj_act_kernel, act=True), h, [w], [qk_tiles + v_tiles], v_tiles,
                      name="proj_c_gate")
    return qk, v, gate


def _proj_b(h, w, tn=1024):
    nt = w.shape[1] // 3 // tn
    u = _proj_cols(_proj_glu_kernel, h, [w, w], [0, nt], nt, name="proj_b_glu")
    gate = _proj_cols(functools.partial(_proj_act_kernel, act=True), h, [w], [2 * nt], nt, name="proj_b_gate")
    return u, gate


def _attn_kernel(q_ref, kc_ref, kp_ref, vc_ref, vp_ref, o_ref, lse_ref):
    blk = pl.program_id(2)
    row = lax.broadcasted_iota(jnp.int32, (A_BLOCK, 2 * A_BLOCK), 0)
    col = lax.broadcasted_iota(jnp.int32, (A_BLOCK, 2 * A_BLOCK), 1)
    keep = (((col < A_BLOCK) & (col >= row) & (blk > 0))
            | ((col >= A_BLOCK) & (col - A_BLOCK <= row)))
    bias = jnp.where(keep, 0.0, -jnp.inf).astype(F32)
    nt = (((1,), (1,)), ((), ()))
    heads = [slice(hd * A_HEAD_DIM, (hd + 1) * A_HEAD_DIM) for hd in range(A_HEADS)]
    scores = []
    for sl in heads:
        k2 = jnp.concatenate([kp_ref[:, sl], kc_ref[:, sl]], axis=0)
        scores.append(lax.dot_general(q_ref[:, sl], k2, nt, preferred_element_type=F32) + bias)
    maxes = [jnp.max(s, axis=1, keepdims=True) for s in scores]
    probs = [jnp.exp(s - mx) for s, mx in zip(scores, maxes)]
    sums = [jnp.sum(p, axis=1, keepdims=True) for p in probs]
    outs = []
    for sl, p in zip(heads, probs):
        v2 = jnp.concatenate([vp_ref[:, sl], vc_ref[:, sl]], axis=0)
        outs.append(jnp.dot(p.astype(BF16), v2, preferred_element_type=F32))
    lse_ref[...] = jnp.zeros(lse_ref.shape, F32)
    for hd, sl in enumerate(heads):
        o_ref[:, sl] = (outs[hd] / sums[hd]).astype(BF16)
        lse_ref[:, hd:hd + 1] = maxes[hd] + jnp.log(sums[hd])


def _attention_group(proj, batch, seq, dil):
    m = batch * seq
    per_batch = seq // A_BLOCK
    nblk = seq // (A_BLOCK * dil)

    def spec(tile, prev):
        if prev:
            return pl.BlockSpec((A_BLOCK, A_WIDTH),
                                lambda b, r, t: (b * per_batch + jnp.maximum(t - 1, 0) * dil + r, tile))
        return pl.BlockSpec((A_BLOCK, A_WIDTH), lambda b, r, t: (b * per_batch + t * dil + r, tile))

    return pl.pallas_call(
        _attn_kernel,
        out_shape=(jax.ShapeDtypeStruct((m, A_WIDTH), BF16), jax.ShapeDtypeStruct((m, LANES), F32)),
        grid=(batch, dil, nblk),
        in_specs=[spec(0, False), spec(1, False), spec(1, True), spec(2, False), spec(2, True)],
        out_specs=(pl.BlockSpec((A_BLOCK, A_WIDTH), lambda b, r, t: (b * per_batch + t * dil + r, 0)),
                   pl.BlockSpec((A_BLOCK, LANES), lambda b, r, t: (b * per_batch + t * dil + r, 0))),
        compiler_params=_params("parallel", "parallel", "arbitrary"),
        name=f"attn_d{dil}",
    )(proj, proj, proj, proj, proj)


def _finish(y, x_ref, nw_ref, out_refs, slab_ref, last, dils):
    x_new = x_ref[...] + y
    h = _rms(x_new, nw_ref[...])
    if last:
        out_refs[0][...] = h
    else:
        out_refs[0][...] = x_new
        _write_h(h, out_refs[1:], slab_ref, dils)


def _tail_outputs(m, d, tm, last, dils, step):
    x_shape = jax.ShapeDtypeStruct((m, d), F32)
    x_spec = pl.BlockSpec((tm, d), lambda *g: (step(*g), 0))
    if last:
        return [x_shape], [x_spec]
    shapes, specs = _h_outputs(m, d, tm, dils, step)
    return [x_shape] + shapes, [x_spec] + specs


def _split_tail(rest, n_scratch, last, dils):
    n_out = 1 if last else 2 + len(dils)
    outs = rest[:n_out]
    slab = rest[n_out] if (dils and not last) else None
    return outs, slab, rest[len(rest) - n_scratch:]


def _gather_residue_major(ref, slab_ref):
    n_sub, dil, rows, c = ref.shape
    for s in range(n_sub):
        for r in range(dil):
            v = ref[s, r].astype(F32)
            for k in range(c // LANES):
                slab_ref[k, pl.ds(s * A_BLOCK * dil + r, rows, stride=dil), :] = v[:, k * LANES:(k + 1) * LANES]


def _out_a_kernel(o1_ref, o2_ref, o3_ref, l1_ref, l2_ref, l3_ref, g_ref, x_ref, w_ref, nw_ref, *rest, last, dils):
    out_refs, slab, (y_ref, os2, os3, ls2, ls3) = _split_tail(rest, 5, last, dils)
    _gather_residue_major(o2_ref, os2)
    _gather_residue_major(o3_ref, os3)
    _gather_residue_major(l2_ref, ls2)
    _gather_residue_major(l3_ref, ls3)
    l1, l2, l3 = l1_ref[:, :A_HEADS], ls2[0, :, :A_HEADS], ls3[0, :, :A_HEADS]
    mx = jnp.maximum(jnp.maximum(l1, l2), l3)
    e1, e2, e3 = jnp.exp(l1 - mx), jnp.exp(l2 - mx), jnp.exp(l3 - mx)
    den = e1 + e2 + e3
    a1, a2, a3 = e1 / den, e2 / den, e3 / den
    for hd in range(A_HEADS):
        sl = slice(hd * A_HEAD_DIM, (hd + 1) * A_HEAD_DIM)
        c = slice(hd, hd + 1)
        o = a1[:, c] * o1_ref[:, sl].astype(F32) + a2[:, c] * os2[hd] + a3[:, c] * os3[hd]
        y_ref[:, sl] = (o * g_ref[:, sl].astype(F32)).astype(BF16)
    y = jnp.dot(y_ref[...], w_ref[...], preferred_element_type=F32)
    _finish(y, x_ref, nw_ref, out_refs, slab, last, dils)


def _residue_major_input(a, tm, dil):
    m, c = a.shape
    shape, block, n_part = _perm_geometry(m, tm, dil)
    spec = pl.BlockSpec(block + (c,), lambda i: (i // n_part, 0, 0, i % n_part, 0, 0))
    return a.reshape(shape + (c,)), spec


def _out_a(os_, lses, proj0, x, w, nw, last, dils, tm=512):
    m, d = x.shape
    row = lambda i: (i, 0)
    o2, o2_spec = _residue_major_input(os_[1], tm, A_DILATIONS[1])
    o3, o3_spec = _residue_major_input(os_[2], tm, A_DILATIONS[2])
    l2, l2_spec = _residue_major_input(lses[1], tm, A_DILATIONS[1])
    l3, l3_spec = _residue_major_input(lses[2], tm, A_DILATIONS[2])
    shapes, specs = _tail_outputs(m, d, tm, last, dils, lambda i: i)
    o_slab = pltpu.VMEM((A_HEADS, tm, LANES), F32)
    l_slab = pltpu.VMEM((1, tm, LANES), F32)
    return pl.pallas_call(
        functools.partial(_out_a_kernel, last=last, dils=dils),
        out_shape=shapes,
        grid=(m // tm,),
        in_specs=[pl.BlockSpec((tm, A_WIDTH), row), o2_spec, o3_spec,
                  pl.BlockSpec((tm, LANES), row), l2_spec, l3_spec,
                  pl.BlockSpec((tm, A_WIDTH), lambda i: (i, 3)),
                  pl.BlockSpec((tm, d), row),
                  _resident((A_WIDTH, d), lambda i: (0, 0)), _resident((1, d), lambda i: (0, 0))],
        out_specs=specs,
        scratch_shapes=([] if last else _slab_scratch(d, tm, dils))
        + [pltpu.VMEM((tm, A_WIDTH), BF16), o_slab, o_slab, l_slab, l_slab],
        compiler_params=_params("parallel"),
        name="out_a",
    )(os_[0], o2, o3, lses[0], l2, l3, proj0, x, w, nw.reshape(1, d))


def _out_b_kernel(u_ref, up_ref, g_ref, x_ref, cw_ref, cb_ref, lw_ref, lb_ref, w_ref, nw_ref, *rest,
                  last, dils, tm, rc):
    out_refs, slab, (ubuf, cbuf, y_ref) = _split_tail(rest, 3, last, dils)
    i = pl.program_id(1)
    n_slab = ubuf.shape[0]
    halo = jnp.where(i > 0, up_ref[...].astype(F32), 0.0)
    for c in range(n_slab):
        sl = slice(c * LANES, (c + 1) * LANES)
        ubuf[c, 0:B_HALO, :] = halo[:, sl]
        ubuf[c, B_HALO:, :] = u_ref[:, sl].astype(F32)
    first = B_HALO - (B_CONV_WIDTH - 1)
    for c in range(n_slab):
        sl = slice(c * LANES, (c + 1) * LANES)
        taps = [cw_ref[j:j + 1, sl] for j in range(B_CONV_WIDTH)]
        bias = cb_ref[:, sl]

        def body(r, carry, c=c, sl=sl, taps=taps, bias=bias):
            r0 = pl.multiple_of(r * rc, rc)
            acc = jnp.broadcast_to(bias, (rc, LANES))
            for j in range(B_CONV_WIDTH):
                acc = acc + ubuf[c, pl.ds(r0 + first + j, rc), :] * taps[j]
            cbuf[pl.ds(r0, rc), sl] = acc
            return carry

        lax.fori_loop(0, tm // rc, body, 0)
    cv = cbuf[...]
    mu = jnp.mean(cv, axis=-1, keepdims=True)
    cen = cv - mu
    var = jnp.mean(cen * cen, axis=-1, keepdims=True)
    cn = cen * lax.rsqrt(var + EPS) * lw_ref[...] + lb_ref[...]
    y_ref[...] = (_silu(cn) * g_ref[...].astype(F32)).astype(BF16)
    y = jnp.dot(y_ref[...], w_ref[...], preferred_element_type=F32)
    _finish(y, x_ref, nw_ref, out_refs, slab, last, dils)


def _out_b(u, gate, x, conv_w, conv_b, ln_w, ln_b, w, nw, batch, seq, last, dils, tm=256, rc=64):
    m, d = x.shape
    width = u.shape[1]
    nt = seq // tm
    hb = tm // B_HALO
    cur = lambda b, i: (b * nt + i, 0)
    vec = lambda a: a.reshape(1, -1)
    const = lambda b, i: (0, 0)
    shapes, specs = _tail_outputs(m, d, tm, last, dils, lambda b, i: b * nt + i)
    return pl.pallas_call(
        functools.partial(_out_b_kernel, last=last, dils=dils, tm=tm, rc=rc),
        out_shape=shapes,
        grid=(batch, nt),
        in_specs=[pl.BlockSpec((tm, width), cur),
                  pl.BlockSpec((B_HALO, width), lambda b, i: (jnp.maximum((b * nt + i) * hb - 1, 0), 0)),
                  pl.BlockSpec((tm, width), cur),
                  pl.BlockSpec((tm, d), cur),
                  _resident((B_CONV_WIDTH, width), const), _resident((1, width), const),
                  _resident((1, width), const), _resident((1, width), const),
                  _resident((width, d), const), _resident((1, d), const)],
        out_specs=specs,
        scratch_shapes=([] if last else _slab_scratch(d, tm, dils))
        + [pltpu.VMEM((width // LANES, tm + B_HALO, LANES), F32),
           pltpu.VMEM((tm, width), F32), pltpu.VMEM((tm, width), BF16)],
        compiler_params=_params("parallel", "arbitrary"),
        name="out_b",
    )(u, u, gate, x, conv_w, vec(conv_b), vec(ln_w), vec(ln_b), w, vec(nw))


def _retention_kernel(q_ref, k_ref, v_ref, g_ref, dm_ref, qd_ref, kd_ref, cd_ref, y_ref, state, *, tc, ch):
    @pl.when(pl.program_id(2) == 0)
    def _():
        state[...] = jnp.zeros(state.shape, F32)

    nt = (((1,), (1,)), ((), ()))
    tn = (((0,), (0,)), ((), ()))
    dm, qd, kd, cd = dm_ref[...], qd_ref[...], kd_ref[...], cd_ref[...]
    n = tc // ch
    rows = [slice(c * ch, (c + 1) * ch) for c in range(n)]
    qs = [q_ref[rs, :] for rs in rows]
    ks = [k_ref[rs, :] for rs in rows]
    vs = [v_ref[rs, :] for rs in rows]
    inner = [(lax.dot_general(q, k, nt, preferred_element_type=F32) * dm).astype(BF16) for q, k in zip(qs, ks)]
    kdec = [(k.astype(F32) * kd).astype(BF16) for k in ks]
    intra = [jnp.dot(a, v, preferred_element_type=F32) for a, v in zip(inner, vs)]
    update = [lax.dot_general(a, v, tn, preferred_element_type=F32) for a, v in zip(kdec, vs)]
    st = state[...]
    for c in range(n):
        cross = jnp.dot(qs[c], st.astype(BF16), preferred_element_type=F32)
        o = intra[c] + cross * qd
        st = st * cd + update[c]
        if c == n - 1:
            state[...] = st
        mu = jnp.mean(o, axis=-1, keepdims=True)
        cen = o - mu
        var = jnp.mean(cen * cen, axis=-1, keepdims=True)
        on = cen * lax.rsqrt(var + EPS)
        y_ref[rows[c], :] = (on * g_ref[rows[c], :].astype(F32)).astype(BF16)


def _retention(qk, v, gate, batch, seq, tc=1024, ch=256):
    m = batch * seq
    nt = seq // tc
    vw = C_HEADS * C_V_DIM

    gammas = 1.0 - jnp.exp(jnp.linspace(math.log(1.0 / 32), math.log(1.0 / 512), C_HEADS, dtype=F32))
    log_g = jnp.log(gammas)
    idx = jnp.arange(ch, dtype=F32)
    diff = idx[:, None] - idx[None, :]
    decay_mask = jnp.where(diff >= 0, jnp.exp(jnp.maximum(diff, 0.0)[None] * log_g[:, None, None]), 0.0)
    q_decay = jnp.exp((idx[None] + 1.0) * log_g[:, None])
    k_decay = jnp.exp((ch - 1.0 - idx[None]) * log_g[:, None])
    chunk_decay = jnp.exp(ch * log_g)
    qd = jnp.broadcast_to(q_decay[:, :, None], (C_HEADS, ch, C_V_DIM))
    kd = jnp.broadcast_to(k_decay[:, :, None], (C_HEADS, ch, C_QK_DIM))
    cd = jnp.broadcast_to(chunk_decay[:, None, None], (C_HEADS, 1, C_V_DIM))

    head = lambda b, h, i: (h, 0, 0)
    rows = lambda off: (lambda b, h, i: (b * nt + i, off + h))
    return pl.pallas_call(
        functools.partial(_retention_kernel, tc=tc, ch=ch),
        out_shape=jax.ShapeDtypeStruct((m, vw), BF16),
        grid=(batch, C_HEADS, nt),
        in_specs=[pl.BlockSpec((tc, C_QK_DIM), rows(0)),
                  pl.BlockSpec((tc, C_QK_DIM), rows(C_HEADS)),
                  pl.BlockSpec((tc, C_V_DIM), rows(0)),
                  pl.BlockSpec((tc, C_V_DIM), rows(0)),
                  pl.BlockSpec((None, ch, ch), head),
                  pl.BlockSpec((None, ch, C_V_DIM), head),
                  pl.BlockSpec((None, ch, C_QK_DIM), head),
                  pl.BlockSpec((None, 1, C_V_DIM), head)],
        out_specs=pl.BlockSpec((tc, C_V_DIM), rows(0)),
        scratch_shapes=[pltpu.VMEM((C_QK_DIM, C_V_DIM), F32)],
        compiler_params=_params("parallel", "parallel", "arbitrary"),
        name="retention",
    )(qk, qk, v, gate, decay_mask, qd, kd, cd)


def _out_c_kernel(y_ref, x_ref, w_ref, nw_ref, *rest, last, dils):
    out_refs, slab, _ = _split_tail(rest, 0, last, dils)
    y = jnp.dot(y_ref[...], w_ref[...], preferred_element_type=F32)
    _finish(y, x_ref, nw_ref, out_refs, slab, last, dils)


def _out_c(y, x, w, nw, last, dils, tm=256):
    m, d = x.shape
    kdim = y.shape[1]
    row = lambda i: (i, 0)
    shapes, specs = _tail_outputs(m, d, tm, last, dils, lambda i: i)
    return pl.pallas_call(
        functools.partial(_out_c_kernel, last=last, dils=dils),
        out_shape=shapes,
        grid=(m // tm,),
        in_specs=[pl.BlockSpec((tm, kdim), row), pl.BlockSpec((tm, d), row),
                  _resident((kdim, d), lambda i: (0, 0)), _resident((1, d), lambda i: (0, 0))],
        out_specs=specs,
        scratch_shapes=[] if last else _slab_scratch(d, tm, dils),
        compiler_params=_params("parallel"),
        name="out_c",
    )(y, x, w, nw.reshape(1, d))


def kernel(x, positions, norm_w, final_norm_w, a_w_in, a_w_out, b_w_in, b_conv_w, b_conv_b, b_ln_w, b_ln_b,
           b_w_out, c_w_in, c_w_out):
    batch, seq, d = x.shape
    depth = norm_w.shape[0]
    m = batch * seq
    xs = x.reshape(m, d)
    perm_dils = tuple(dl for dl in A_DILATIONS if dl > 1)

    def next_dils(i):
        return perm_dils if (i + 1 < depth and (i + 1) % 3 == 0) else ()

    fa, fc = _rope_freqs()
    pos = positions.reshape(m, 1)
    tabs_a = [_rope_tables(_rope_a_kernel, 3, pos if dl == 1 else _to_residue_major(pos, batch, seq, dl), fa,
                           f"rope_a{dl}") for dl in A_DILATIONS]
    tabs_c = _rope_tables(_rope_c_kernel, 2, pos, fc, "rope_c")

    hs = _rmsnorm(xs, norm_w[0], perm_dils)
    for i in range(depth):
        kind, j = i % 3, i // 3
        last = i == depth - 1
        nw = final_norm_w if last else norm_w[i + 1]
        dils = next_dils(i)
        if kind == 0:
            w_in = a_w_in[j].astype(BF16)
            os_, lses, proj0 = [], [], None
            for g, dl in enumerate(A_DILATIONS):
                h_g = hs[g].reshape(m, d)
                proj = _proj_a(h_g, w_in, tabs_a[g], g)
                proj0 = proj if g == 0 else proj0
                o, lse = _attention_group(proj, batch, seq, dl)
                os_.append(o)
                lses.append(lse)
            outs = _out_a(os_, lses, proj0, xs, a_w_out[j].astype(BF16), nw, last, dils)
        elif kind == 1:
            u, gate = _proj_b(hs[0], b_w_in[j].astype(BF16))
            outs = _out_b(u, gate, xs, b_conv_w[j], b_conv_b[j], b_ln_w[j], b_ln_b[j],
                          b_w_out[j].astype(BF16), nw, batch, seq, last, dils)
        else:
            qk, v, gate = _proj_c(hs[0], c_w_in[j].astype(BF16), tabs_c)
            y = _retention(qk, v, gate, batch, seq)
            outs = _out_c(y, xs, c_w_out[j].astype(BF16), nw, last, dils)
        if last:
            return outs[0].reshape(batch, seq, d)
        xs, hs = outs[0], outs[1:]
```

```python
import functools
import math

import jax
import jax.numpy as jnp
from jax import lax
from jax.experimental import pallas as pl
from jax.experimental.pallas import tpu as pltpu

F32 = jnp.float32
BF16 = jnp.bfloat16

EPS = 1e-6
LANES = 128
MXU_COLS = 256
V7X_VMEM_LIMIT = 56 * 1024 * 1024

A_DILATIONS = (1, 4, 16)
A_HEADS = 8
A_HEAD_DIM = 128
A_WIDTH = A_HEADS * A_HEAD_DIM
A_ROT_DIM = A_HEAD_DIM // 4
A_ROPE_THETA = 500000.0
A_BLOCK = 128
B_CONV_WIDTH = 31
B_HALO = 32
C_QK_DIM = 256
C_HEADS = 8
C_V_DIM = 512
C_CHUNK = 128
C_ROPE_THETA = 10000.0


def _params(*sem):
    return pltpu.CompilerParams(dimension_semantics=sem, vmem_limit_bytes=V7X_VMEM_LIMIT)


def _resident(shape, index_map):
    return pl.BlockSpec(shape, index_map, pipeline_mode=pl.Buffered(1))


def _rms(x, w):
    y = x * lax.rsqrt(jnp.mean(x * x, axis=-1, keepdims=True) + EPS)
    return y * w


def _silu(x):
    return x * jax.nn.sigmoid(x)


def _perm_geometry(m, tm, dil):
    tile = A_BLOCK * dil
    big = max(tm, tile)
    n_sub, n_part, rows = big // tile, big // tm, min(tm, tile) // dil
    return (m // big, n_sub, dil, n_part, rows), (None, n_sub, dil, None, rows), n_part


def _perm_outputs(m, c, dtype, tm, dils, step):
    shapes = [jax.ShapeDtypeStruct((m, c), dtype)]
    specs = [pl.BlockSpec((tm, c), lambda *g: (step(*g), 0))]
    for dil in dils:
        shape, block, n_part = _perm_geometry(m, tm, dil)
        shapes.append(jax.ShapeDtypeStruct(shape + (c,), dtype))
        specs.append(pl.BlockSpec(block + (c,),
                                  lambda *g, n_part=n_part: (step(*g) // n_part, 0, 0, step(*g) % n_part, 0, 0)))
    return shapes, specs


def _write_perms(v, refs, slab_ref, dils):
    dtype = refs[0].dtype
    refs[0][...] = v.astype(dtype)
    if not dils:
        return
    n_slab = slab_ref.shape[0]
    for c in range(n_slab):
        slab_ref[c] = v[:, c * LANES:(c + 1) * LANES]
    for ref, dil in zip(refs[1:], dils):
        n_sub, _, rows, _ = ref.shape
        for s in range(n_sub):
            for r in range(dil):
                for c in range(n_slab):
                    t = slab_ref[c, pl.ds(s * A_BLOCK * dil + r, rows, stride=dil), :]
                    ref[s, r, :, c * LANES:(c + 1) * LANES] = t.astype(dtype)


def _slab_scratch(d, tm, dils):
    return [pltpu.VMEM((d // LANES, tm, LANES), F32)] if dils else []


def _rmsnorm_kernel(x_ref, w_ref, *rest, dils):
    h_refs, slab = (rest[:-1], rest[-1]) if dils else (rest, None)
    _write_perms(_rms(x_ref[...], w_ref[...]), h_refs, slab, dils)


def _rmsnorm(x, w, dils, tm=512):
    m, d = x.shape
    shapes, specs = _perm_outputs(m, d, BF16, tm, dils, lambda i: i)
    return pl.pallas_call(
        functools.partial(_rmsnorm_kernel, dils=dils),
        out_shape=shapes,
        grid=(m // tm,),
        in_specs=[pl.BlockSpec((tm, d), lambda i: (i, 0)), _resident((1, d), lambda i: (0, 0))],
        out_specs=specs,
        scratch_shapes=_slab_scratch(d, tm, dils),
        compiler_params=_params("parallel"),
        name="rmsnorm_in",
    )(x, w.reshape(1, d))


def _rope_a_kernel(pos_ref, f_ref, *rest, dils):
    outs, slab = rest[:-1], rest[-1]
    ang = pos_ref[...].astype(F32) * f_ref[...]
    sin = jnp.sin(ang)
    lane = lax.broadcasted_iota(jnp.int32, ang.shape, 1)
    half = A_ROT_DIM // 2
    tables = (jnp.cos(ang),
              jnp.where(lane < half, -sin, 0.0),
              jnp.where((lane >= half) & (lane < A_ROT_DIM), sin, 0.0))
    n = 1 + len(dils)
    for k, tab in enumerate(tables):
        _write_perms(tab, outs[k * n:(k + 1) * n], slab, dils)


def _rope_c_kernel(pos_ref, f_ref, cos_ref, sin_ref):
    ang = pos_ref[...].astype(F32) * f_ref[...]
    cos_ref[...] = jnp.cos(ang)
    sin_ref[...] = jnp.sin(ang)


def _rope_a_tables(pos, freqs, dils, tm=1024):
    m = pos.shape[0]
    shapes, specs = _perm_outputs(m, LANES, F32, tm, dils, lambda i: i)
    outs = pl.pallas_call(
        functools.partial(_rope_a_kernel, dils=dils),
        out_shape=shapes * 3,
        grid=(m // tm,),
        in_specs=[pl.BlockSpec((tm, 1), lambda i: (i, 0)), _resident((1, LANES), lambda i: (0, 0))],
        out_specs=specs * 3,
        scratch_shapes=[pltpu.VMEM((1, tm, LANES), F32)],
        compiler_params=_params("parallel"),
        name="rope_a",
    )(pos, freqs)
    n = 1 + len(dils)
    return [tuple(outs[k * n + o].reshape(m, LANES) for k in range(3)) for o in range(n)]


def _rope_c_tables(pos, freqs, tm=1024):
    m = pos.shape[0]
    row = pl.BlockSpec((tm, LANES), lambda i: (i, 0))
    return pl.pallas_call(
        _rope_c_kernel,
        out_shape=(jax.ShapeDtypeStruct((m, LANES), F32),) * 2,
        grid=(m // tm,),
        in_specs=[pl.BlockSpec((tm, 1), lambda i: (i, 0)), _resident((1, LANES), lambda i: (0, 0))],
        out_specs=(row, row),
        compiler_params=_params("parallel"),
        name="rope_c",
    )(pos, freqs)


def _rope_freqs():
    half_a = A_ROT_DIM // 2
    inv_a = A_ROPE_THETA ** (-jnp.arange(0, A_ROT_DIM, 2, dtype=F32) / A_ROT_DIM)
    fa = jnp.concatenate([inv_a, inv_a, jnp.zeros((LANES - 2 * half_a,), F32)]).reshape(1, LANES)
    fc = (C_ROPE_THETA ** (-jnp.arange(0, C_QK_DIM, 2, dtype=F32) / C_QK_DIM)).reshape(1, LANES)
    return fa, fc


def _rotary_a(t, cos, slo, shi):
    return (t * cos + pltpu.roll(t, LANES - A_ROT_DIM // 2, 1) * slo + pltpu.roll(t, A_ROT_DIM // 2, 1) * shi)


def _proj_a_kernel(h_ref, wqkv_ref, *rest, has_gate):
    if has_gate:
        wg_ref, cos_ref, slo_ref, shi_ref, o_ref = rest
    else:
        wg_ref, (cos_ref, slo_ref, shi_ref, o_ref) = None, rest
    h = h_ref[...]
    cos, slo, shi = cos_ref[...], slo_ref[...], shi_ref[...]
    scale = A_HEAD_DIM ** -0.5
    q_tabs = (cos * scale, slo * scale, shi * scale)
    for c in range(o_ref.shape[1] // MXU_COLS):
        tile, off = divmod(c * MXU_COLS, A_WIDTH)
        w_chunk = (wg_ref[:, off:off + MXU_COLS] if tile == 3
                   else wqkv_ref[:, c * MXU_COLS:(c + 1) * MXU_COLS])
        acc = jnp.dot(h, w_chunk, preferred_element_type=F32)
        for k in range(MXU_COLS // A_HEAD_DIM):
            t = acc[:, k * A_HEAD_DIM:(k + 1) * A_HEAD_DIM]
            if tile == 0:
                t = _rotary_a(t, *q_tabs)
            elif tile == 1:
                t = _rotary_a(t, cos, slo, shi)
            elif tile == 3:
                t = _silu(t)
            lo = c * MXU_COLS + k * A_HEAD_DIM
            o_ref[:, lo:lo + A_HEAD_DIM] = t.astype(BF16)


def _proj_a(h, w, tabs, g, tm=1024):
    m, d = h.shape
    has_gate = g == 0
    n_out = (4 if has_gate else 3) * A_WIDTH
    gate_tile = w.shape[1] // A_WIDTH - 1
    tab = pl.BlockSpec((tm, LANES), lambda i: (i, 0))
    w_specs = [_resident((d, 3 * A_WIDTH), lambda i: (0, g))]
    w_args = [w]
    if has_gate:
        w_specs.append(_resident((d, A_WIDTH), lambda i: (0, gate_tile)))
        w_args.append(w)
    return pl.pallas_call(
        functools.partial(_proj_a_kernel, has_gate=has_gate),
        out_shape=jax.ShapeDtypeStruct((m, n_out), BF16),
        grid=(m // tm,),
        in_specs=[pl.BlockSpec((tm, d), lambda i: (i, 0))] + w_specs + [tab, tab, tab],
        out_specs=pl.BlockSpec((tm, n_out), lambda i: (i, 0)),
        compiler_params=_params("parallel"),
        name=f"proj_a{g}",
    )(h, *w_args, *tabs)


def _proj_c_qk_kernel(h_ref, w_ref, cos_ref, sin_ref, o_ref):
    cos, sin = cos_ref[...], sin_ref[...]
    scale = C_QK_DIM ** -0.5
    k_tabs = (cos * scale, sin * scale)
    h = h_ref[...]
    half = C_QK_DIM // 2
    for c in range(o_ref.shape[1] // C_QK_DIM):
        acc = jnp.dot(h, w_ref[:, c * C_QK_DIM:(c + 1) * C_QK_DIM], preferred_element_type=F32)
        t1, t2 = acc[:, :half], acc[:, half:]
        cs, sn = (cos, sin) if c < C_HEADS else k_tabs
        o_ref[:, c * C_QK_DIM:c * C_QK_DIM + half] = (t1 * cs - t2 * sn).astype(BF16)
        o_ref[:, c * C_QK_DIM + half:(c + 1) * C_QK_DIM] = (t2 * cs + t1 * sn).astype(BF16)


def _proj_glu_kernel(h_ref, wa_ref, wb_ref, u_ref):
    h = h_ref[...]
    for c in range(u_ref.shape[1] // MXU_COLS):
        sl = slice(c * MXU_COLS, (c + 1) * MXU_COLS)
        a = jnp.dot(h, wa_ref[:, sl], preferred_element_type=F32)
        b = jnp.dot(h, wb_ref[:, sl], preferred_element_type=F32)
        u_ref[:, sl] = (a * jax.nn.sigmoid(b)).astype(BF16)


def _proj_act_kernel(h_ref, w_ref, o_ref, *, act):
    h = h_ref[...]
    for c in range(o_ref.shape[1] // MXU_COLS):
        sl = slice(c * MXU_COLS, (c + 1) * MXU_COLS)
        acc = jnp.dot(h, w_ref[:, sl], preferred_element_type=F32)
        o_ref[:, sl] = (_silu(acc) if act else acc).astype(BF16)


def _proj_cols(body, h, ws, first_tiles, tn, extra=(), name="proj", tm=1024):
    m, d = h.shape
    tab = pl.BlockSpec((tm, LANES), lambda i: (i, 0))
    w_specs = [_resident((d, tn), lambda i, f=f: (0, f)) for f in first_tiles]
    return pl.pallas_call(
        body,
        out_shape=jax.ShapeDtypeStruct((m, tn), BF16),
        grid=(m // tm,),
        in_specs=[pl.BlockSpec((tm, d), lambda i: (i, 0))] + w_specs + [tab] * len(extra),
        out_specs=pl.BlockSpec((tm, tn), lambda i: (i, 0)),
        compiler_params=_params("parallel"),
        name=name,
    )(h, *ws, *extra)


def _proj_c(h, w, tabs):
    tn = 2 * C_HEADS * C_QK_DIM
    qk = _proj_cols(_proj_c_qk_kernel, h, [w], [0], tn, tabs, "proj_c_qk")
    v = _proj_cols(functools.partial(_proj_act_kernel, act=False), h, [w], [1], tn, name="proj_c_v")
    gate = _proj_cols(functools.partial(_proj_act_kernel, act=True), h, [w], [2], tn, name="proj_c_gate")
    return qk, v, gate


def _proj_b_glu(h, w):
    return _proj_cols(_proj_glu_kernel, h, [w, w], [0, 1], w.shape[1] // 3, name="proj_b_glu")


def _attn_kernel(q_ref, k_ref, kp_ref, v_ref, vp_ref, o_ref, lse_ref, *, nb):
    first = pl.program_id(2) == 0
    row = lax.broadcasted_iota(jnp.int32, (A_BLOCK, 2 * A_BLOCK), 0)
    col = lax.broadcasted_iota(jnp.int32, (A_BLOCK, 2 * A_BLOCK), 1)
    in_prev = (col < A_BLOCK) & (col >= row)
    in_cur = (col >= A_BLOCK) & (col - A_BLOCK <= row)
    bias_inner = jnp.where(in_prev | in_cur, 0.0, -jnp.inf).astype(F32)
    bias_first = jnp.where((in_prev & jnp.logical_not(first)) | in_cur, 0.0, -jnp.inf).astype(F32)
    nt = (((1,), (1,)), ((), ()))
    heads = [slice(hd * A_HEAD_DIM, (hd + 1) * A_HEAD_DIM) for hd in range(A_HEADS)]
    items = [(g, hd) for g in range(nb) for hd in range(A_HEADS)]

    def keys(ref, prev_ref, g, sl):
        prev = prev_ref[:, sl] if g == 0 else ref[g - 1, :, sl]
        return jnp.concatenate([prev, ref[g, :, sl]], axis=0)

    scores = [lax.dot_general(q_ref[g, :, heads[hd]], keys(k_ref, kp_ref, g, heads[hd]), nt,
                              preferred_element_type=F32) + (bias_first if g == 0 else bias_inner)
              for g, hd in items]
    maxes = [jnp.max(s, axis=1, keepdims=True) for s in scores]
    probs = [jnp.exp(s - mx) for s, mx in zip(scores, maxes)]
    sums = [jnp.sum(p, axis=1, keepdims=True) for p in probs]
    outs = [jnp.dot(p.astype(BF16), keys(v_ref, vp_ref, g, heads[hd]), preferred_element_type=F32)
            for (g, hd), p in zip(items, probs)]
    lse_ref[...] = jnp.zeros(lse_ref.shape, F32)
    for n, (g, hd) in enumerate(items):
        o_ref[g, :, heads[hd]] = (outs[n] / sums[n]).astype(BF16)
        lse_ref[g, :, hd:hd + 1] = maxes[n] + jnp.log(sums[n])


def _attention_group(proj, batch, seq, dil, nb=4):
    m = batch * seq
    nblk = seq // (A_BLOCK * dil)
    nb = math.gcd(nb, nblk)
    steps = nblk // nb
    view = proj.reshape(batch * nblk, dil, A_BLOCK, proj.shape[1])

    def cur(tile):
        return pl.BlockSpec((nb, None, A_BLOCK, A_WIDTH), lambda b, r, t: (b * steps + t, r, 0, tile))

    def prev(tile):
        return pl.BlockSpec((None, None, A_BLOCK, A_WIDTH),
                            lambda b, r, t: (b * nblk + jnp.maximum(t * nb - 1, 0), r, 0, tile))

    o, lse = pl.pallas_call(
        functools.partial(_attn_kernel, nb=nb),
        out_shape=(jax.ShapeDtypeStruct((batch * nblk, dil, A_BLOCK, A_WIDTH), BF16),
                   jax.ShapeDtypeStruct((batch * nblk, dil, A_BLOCK, LANES), F32)),
        grid=(batch, dil, steps),
        in_specs=[cur(0), cur(1), prev(1), cur(2), prev(2)],
        out_specs=(pl.BlockSpec((nb, None, A_BLOCK, A_WIDTH), lambda b, r, t: (b * steps + t, r, 0, 0)),
                   pl.BlockSpec((nb, None, A_BLOCK, LANES), lambda b, r, t: (b * steps + t, r, 0, 0))),
        compiler_params=_params("parallel", "parallel", "arbitrary"),
        name=f"attn_d{dil}",
    )(view, view, view, view, view)
    return o.reshape(m, A_WIDTH), lse.reshape(m, LANES)


def _finish(y, x_ref, nw_ref, out_refs, slab_ref, last, dils):
    x_new = x_ref[...] + y
    h = _rms(x_new, nw_ref[...])
    if last:
        out_refs[0][...] = h
    else:
        out_refs[0][...] = x_new
        _write_perms(h, out_refs[1:], slab_ref, dils)


def _tail_outputs(m, d, tm, last, dils, step):
    x_shape = jax.ShapeDtypeStruct((m, d), F32)
    x_spec = pl.BlockSpec((tm, d), lambda *g: (step(*g), 0))
    if last:
        return [x_shape], [x_spec]
    shapes, specs = _perm_outputs(m, d, BF16, tm, dils, step)
    return [x_shape] + shapes, [x_spec] + specs


def _split_tail(rest, n_scratch, last, dils):
    n_out = 1 if last else 2 + len(dils)
    outs = rest[:n_out]
    slab = rest[n_out] if (dils and not last) else None
    return outs, slab, rest[len(rest) - n_scratch:]


def _gather_residue_major(ref, slab_ref):
    n_sub, dil, rows, c = ref.shape
    for s in range(n_sub):
        for r in range(dil):
            v = ref[s, r].astype(F32)
            for k in range(c // LANES):
                slab_ref[k, pl.ds(s * A_BLOCK * dil + r, rows, stride=dil), :] = v[:, k * LANES:(k + 1) * LANES]


def _out_a_kernel(o1_ref, o2_ref, o3_ref, l1_ref, l2_ref, l3_ref, g_ref, x_ref, w_ref, nw_ref, *rest, last, dils):
    out_refs, slab, (y_ref, os2, os3, ls2, ls3) = _split_tail(rest, 5, last, dils)
    _gather_residue_major(o2_ref, os2)
    _gather_residue_major(o3_ref, os3)
    _gather_residue_major(l2_ref, ls2)
    _gather_residue_major(l3_ref, ls3)
    l1, l2, l3 = l1_ref[:, :A_HEADS], ls2[0, :, :A_HEADS], ls3[0, :, :A_HEADS]
    mx = jnp.maximum(jnp.maximum(l1, l2), l3)
    e1, e2, e3 = jnp.exp(l1 - mx), jnp.exp(l2 - mx), jnp.exp(l3 - mx)
    den = e1 + e2 + e3
    a1, a2, a3 = e1 / den, e2 / den, e3 / den
    for hd in range(A_HEADS):
        sl = slice(hd * A_HEAD_DIM, (hd + 1) * A_HEAD_DIM)
        c = slice(hd, hd + 1)
        o = a1[:, c] * o1_ref[:, sl].astype(F32) + a2[:, c] * os2[hd] + a3[:, c] * os3[hd]
        y_ref[:, sl] = (o * g_ref[:, sl].astype(F32)).astype(BF16)
    y = jnp.dot(y_ref[...], w_ref[...], preferred_element_type=F32)
    _finish(y, x_ref, nw_ref, out_refs, slab, last, dils)


def _residue_major_input(a, tm, dil):
    m, c = a.shape
    shape, block, n_part = _perm_geometry(m, tm, dil)
    spec = pl.BlockSpec(block + (c,), lambda i: (i // n_part, 0, 0, i % n_part, 0, 0))
    return a.reshape(shape + (c,)), spec


def _out_a(os_, lses, proj0, x, w, nw, last, dils, tm=512):
    m, d = x.shape
    row = lambda i: (i, 0)
    o2, o2_spec = _residue_major_input(os_[1], tm, A_DILATIONS[1])
    o3, o3_spec = _residue_major_input(os_[2], tm, A_DILATIONS[2])
    l2, l2_spec = _residue_major_input(lses[1], tm, A_DILATIONS[1])
    l3, l3_spec = _residue_major_input(lses[2], tm, A_DILATIONS[2])
    shapes, specs = _tail_outputs(m, d, tm, last, dils, lambda i: i)
    o_slab = pltpu.VMEM((A_HEADS, tm, LANES), F32)
    l_slab = pltpu.VMEM((1, tm, LANES), F32)
    return pl.pallas_call(
        functools.partial(_out_a_kernel, last=last, dils=dils),
        out_shape=shapes,
        grid=(m // tm,),
        in_specs=[pl.BlockSpec((tm, A_WIDTH), row), o2_spec, o3_spec,
                  pl.BlockSpec((tm, LANES), row), l2_spec, l3_spec,
                  pl.BlockSpec((tm, A_WIDTH), lambda i: (i, 3)),
                  pl.BlockSpec((tm, d), row),
                  _resident((A_WIDTH, d), lambda i: (0, 0)), _resident((1, d), lambda i: (0, 0))],
        out_specs=specs,
        scratch_shapes=([] if last else _slab_scratch(d, tm, dils))
        + [pltpu.VMEM((tm, A_WIDTH), BF16), o_slab, o_slab, l_slab, l_slab],
        compiler_params=_params("parallel"),
        name="out_a",
    )(os_[0], o2, o3, lses[0], l2, l3, proj0, x, w, nw.reshape(1, d))


def _order_after(*values):
    bits = None
    for v in values:
        b = pltpu.bitcast(v[v.shape[0] - 8:, :LANES], jnp.uint32)
        bits = b if bits is None else bits | b
    sixteen = jnp.uint32(16)
    cleared = lax.shift_right_logical(lax.shift_right_logical(bits, sixteen), sixteen)
    return pltpu.bitcast(cleared, F32)[0:1, :]


def _out_b_kernel(u_ref, up_ref, h_ref, x_ref, wg_ref, cw_ref, cb_ref, lw_ref, lb_ref, w_ref, nw_ref, *rest,
                  last, tm, rc, tiles_per_seq, n_tiles):
    out_refs, (ubuf, cbuf, gbuf, y_ref, acc_ref) = rest[:-5], rest[-5:]
    s = pl.program_id(0)

    @pl.when(s == 0)
    def _():
        y_ref[...] = jnp.zeros(y_ref.shape, BF16)

    tile = jnp.minimum(s, n_tiles - 1)
    n_slab = ubuf.shape[0]
    halo = jnp.where(tile % tiles_per_seq > 0, up_ref[...].astype(F32), 0.0)
    for c in range(n_slab):
        sl = slice(c * LANES, (c + 1) * LANES)
        ubuf[c, 0:B_HALO, :] = halo[:, sl]
        ubuf[c, B_HALO:, :] = u_ref[:, sl].astype(F32)
    first = B_HALO - (B_CONV_WIDTH - 1)
    h = h_ref[...]
    y_prev = y_ref[...]
    per = MXU_COLS // LANES
    zero = jnp.zeros((1, LANES), F32)
    for cc in range(n_slab // per):
        cols = slice(cc * MXU_COLS, (cc + 1) * MXU_COLS)
        gate = jnp.dot(h, wg_ref[:, cols], preferred_element_type=F32)
        proj = jnp.dot(y_prev, w_ref[:, cols], preferred_element_type=F32)
        gbuf[:, cols] = _silu(gate).astype(BF16)
        acc_ref[:, cols] = proj
        for c in range(cc * per, (cc + 1) * per):
            sl = slice(c * LANES, (c + 1) * LANES)
            taps = [cw_ref[j:j + 1, sl] for j in range(B_CONV_WIDTH)]
            for r0 in range(0, tm, rc):
                acc = jnp.broadcast_to(cb_ref[:, sl] + zero, (rc, LANES))
                for j in range(B_CONV_WIDTH):
                    acc = acc + ubuf[c, r0 + first + j:r0 + first + j + rc, :] * taps[j]
                cbuf[r0:r0 + rc, sl] = acc
        zero = _order_after(gate, proj)
    x_new = x_ref[...] + acc_ref[...]
    h_new = _rms(x_new, nw_ref[...])
    if last:
        out_refs[0][...] = h_new
    else:
        out_refs[0][...] = x_new
        out_refs[1][...] = h_new.astype(BF16)
    cv = cbuf[...]
    mu = jnp.mean(cv, axis=-1, keepdims=True)
    cen = cv - mu
    var = jnp.mean(cen * cen, axis=-1, keepdims=True)
    cn = cen * lax.rsqrt(var + EPS) * lw_ref[...] + lb_ref[...]
    y_ref[...] = (_silu(cn) * gbuf[...].astype(F32)).astype(BF16)


def _out_b(u, h, x, w_in, conv_w, conv_b, ln_w, ln_b, w, nw, seq, last, tm=256, rc=64):
    m, d = x.shape
    width = u.shape[1]
    tiles_per_seq = seq // tm
    n_tiles = m // tm
    hb = tm // B_HALO
    cur = lambda s: (jnp.minimum(s, n_tiles - 1), 0)
    lag = lambda s: (jnp.maximum(s - 1, 0), 0)
    vec = lambda a: a.reshape(1, -1)
    const = lambda s: (0, 0)
    shapes, specs = _tail_outputs(m, d, tm, last, (), lambda s: jnp.maximum(s - 1, 0))
    return pl.pallas_call(
        functools.partial(_out_b_kernel, last=last, tm=tm, rc=rc, tiles_per_seq=tiles_per_seq, n_tiles=n_tiles),
        out_shape=shapes,
        grid=(n_tiles + 1,),
        in_specs=[pl.BlockSpec((tm, width), cur),
                  pl.BlockSpec((B_HALO, width), lambda s: (jnp.maximum(jnp.minimum(s, n_tiles - 1) * hb - 1, 0), 0)),
                  pl.BlockSpec((tm, d), cur),
                  pl.BlockSpec((tm, d), lag),
                  _resident((d, width), lambda s: (0, 2)),
                  _resident((B_CONV_WIDTH, width), const), _resident((1, width), const),
                  _resident((1, width), const), _resident((1, width), const),
                  _resident((width, d), const), _resident((1, d), const)],
        out_specs=specs,
        scratch_shapes=[pltpu.VMEM((width // LANES, tm + B_HALO, LANES), F32),
                        pltpu.VMEM((tm, width), F32), pltpu.VMEM((tm, width), BF16),
                        pltpu.VMEM((tm, width), BF16), pltpu.VMEM((tm, d), F32)],
        compiler_params=_params("arbitrary"),
        name="out_b",
    )(u, u, h, x, w_in, conv_w, vec(conv_b), vec(ln_w), vec(ln_b), w, vec(nw))


def _retention_kernel(q_ref, k_ref, v_ref, g_ref, dm_ref, qd_ref, kd_ref, cd_ref, y_ref, state, *, tc, ch):
    @pl.when(pl.program_id(2) == 0)
    def _():
        state[...] = jnp.zeros(state.shape, F32)

    nt = (((1,), (1,)), ((), ()))
    tn = (((0,), (0,)), ((), ()))
    dm, qd, kd, cd = dm_ref[...], qd_ref[...], kd_ref[...], cd_ref[...]
    n = tc // ch
    rows = [slice(c * ch, (c + 1) * ch) for c in range(n)]
    qs = [q_ref[rs, :] for rs in rows]
    ks = [k_ref[rs, :] for rs in rows]
    vs = [v_ref[rs, :] for rs in rows]
    inner = [(lax.dot_general(q, k, nt, preferred_element_type=F32) * dm).astype(BF16) for q, k in zip(qs, ks)]
    kdec = [(k.astype(F32) * kd).astype(BF16) for k in ks]
    intra = [jnp.dot(a, v, preferred_element_type=F32) for a, v in zip(inner, vs)]
    update = [lax.dot_general(a, v, tn, preferred_element_type=F32) for a, v in zip(kdec, vs)]
    st = state[...]
    for c in range(n):
        cross = jnp.dot(qs[c], st.astype(BF16), preferred_element_type=F32)
        o = intra[c] + cross * qd
        st = st * cd + update[c]
        if c == n - 1:
            state[...] = st
        mu = jnp.mean(o, axis=-1, keepdims=True)
        cen = o - mu
        var = jnp.mean(cen * cen, axis=-1, keepdims=True)
        on = cen * lax.rsqrt(var + EPS)
        y_ref[rows[c], :] = (on * g_ref[rows[c], :].astype(F32)).astype(BF16)


def _retention(qk, v, gate, batch, seq, tc=1024, ch=256):
    m = batch * seq
    nt = seq // tc
    vw = C_HEADS * C_V_DIM

    gammas = 1.0 - jnp.exp(jnp.linspace(math.log(1.0 / 32), math.log(1.0 / 512), C_HEADS, dtype=F32))
    log_g = jnp.log(gammas)
    idx = jnp.arange(ch, dtype=F32)
    diff = idx[:, None] - idx[None, :]
    decay_mask = jnp.where(diff >= 0, jnp.exp(jnp.maximum(diff, 0.0)[None] * log_g[:, None, None]), 0.0)
    q_decay = jnp.exp((idx[None] + 1.0) * log_g[:, None])
    k_decay = jnp.exp((ch - 1.0 - idx[None]) * log_g[:, None])
    chunk_decay = jnp.exp(ch * log_g)
    qd = jnp.broadcast_to(q_decay[:, :, None], (C_HEADS, ch, C_V_DIM))
    kd = jnp.broadcast_to(k_decay[:, :, None], (C_HEADS, ch, C_QK_DIM))
    cd = jnp.broadcast_to(chunk_decay[:, None, None], (C_HEADS, 1, C_V_DIM))

    head = lambda b, h, i: (h, 0, 0)
    rows = lambda off: (lambda b, h, i: (b * nt + i, off + h))
    return pl.pallas_call(
        functools.partial(_retention_kernel, tc=tc, ch=ch),
        out_shape=jax.ShapeDtypeStruct((m, vw), BF16),
        grid=(batch, C_HEADS, nt),
        in_specs=[pl.BlockSpec((tc, C_QK_DIM), rows(0)),
                  pl.BlockSpec((tc, C_QK_DIM), rows(C_HEADS)),
                  pl.BlockSpec((tc, C_V_DIM), rows(0)),
                  pl.BlockSpec((tc, C_V_DIM), rows(0)),
                  pl.BlockSpec((None, ch, ch), head),
                  pl.BlockSpec((None, ch, C_V_DIM), head),
                  pl.BlockSpec((None, ch, C_QK_DIM), head),
                  pl.BlockSpec((None, 1, C_V_DIM), head)],
        out_specs=pl.BlockSpec((tc, C_V_DIM), rows(0)),
        scratch_shapes=[pltpu.VMEM((C_QK_DIM, C_V_DIM), F32)],
        compiler_params=_params("parallel", "parallel", "arbitrary"),
        name="retention",
    )(qk, qk, v, gate, decay_mask, qd, kd, cd)


def _out_c_kernel(y_ref, x_ref, w_ref, nw_ref, *rest, last, dils):
    out_refs, slab, _ = _split_tail(rest, 0, last, dils)
    y = jnp.dot(y_ref[...], w_ref[...], preferred_element_type=F32)
    _finish(y, x_ref, nw_ref, out_refs, slab, last, dils)


def _out_c(y, x, w, nw, last, dils, tm=256):
    m, d = x.shape
    kdim = y.shape[1]
    row = lambda i: (i, 0)
    shapes, specs = _tail_outputs(m, d, tm, last, dils, lambda i: i)
    return pl.pallas_call(
        functools.partial(_out_c_kernel, last=last, dils=dils),
        out_shape=shapes,
        grid=(m // tm,),
        in_specs=[pl.BlockSpec((tm, kdim), row), pl.BlockSpec((tm, d), row),
                  _resident((kdim, d), lambda i: (0, 0)), _resident((1, d), lambda i: (0, 0))],
        out_specs=specs,
        scratch_shapes=[] if last else _slab_scratch(d, tm, dils),
        compiler_params=_params("parallel"),
        name="out_c",
    )(y, x, w, nw.reshape(1, d))


def kernel(x, positions, norm_w, final_norm_w, a_w_in, a_w_out, b_w_in, b_conv_w, b_conv_b, b_ln_w, b_ln_b,
           b_w_out, c_w_in, c_w_out):
    batch, seq, d = x.shape
    depth = norm_w.shape[0]
    m = batch * seq
    xs = x.reshape(m, d)
    perm_dils = tuple(dl for dl in A_DILATIONS if dl > 1)

    def next_dils(i):
        return perm_dils if (i + 1 < depth and (i + 1) % 3 == 0) else ()

    fa, fc = _rope_freqs()
    pos = positions.reshape(m, 1)
    tabs_a = _rope_a_tables(pos, fa, perm_dils)
    tabs_c = _rope_c_tables(pos, fc)

    hs = _rmsnorm(xs, norm_w[0], perm_dils)
    for i in range(depth):
        kind, j = i % 3, i // 3
        last = i == depth - 1
        nw = final_norm_w if last else norm_w[i + 1]
        dils = next_dils(i)
        if kind == 0:
            w_in = a_w_in[j].astype(BF16)
            os_, lses, proj0 = [], [], None
            for g, dl in enumerate(A_DILATIONS):
                h_g = hs[g].reshape(m, d)
                proj = _proj_a(h_g, w_in, tabs_a[g], g)
                proj0 = proj if g == 0 else proj0
                o, lse = _attention_group(proj, batch, seq, dl)
                os_.append(o)
                lses.append(lse)
            outs = _out_a(os_, lses, proj0, xs, a_w_out[j].astype(BF16), nw, last, dils)
        elif kind == 1:
            assert not dils
            w_in = b_w_in[j].astype(BF16)
            u = _proj_b_glu(hs[0], w_in)
            outs = _out_b(u, hs[0], xs, w_in, b_conv_w[j], b_conv_b[j], b_ln_w[j], b_ln_b[j],
                          b_w_out[j].astype(BF16), nw, seq, last)
        else:
            qk, v, gate = _proj_c(hs[0], c_w_in[j].astype(BF16), tabs_c)
            y = _retention(qk, v, gate, batch, seq)
            outs = _out_c(y, xs, c_w_out[j].astype(BF16), nw, last, dils)
        if last:
            return outs[0].reshape(batch, seq, d)
        xs, hs = outs[0], outs[1:]
```

```python
import functools
import math

import jax
import jax.numpy as jnp
from jax import lax
from jax.experimental import pallas as pl
from jax.experimental.pallas import tpu as pltpu

F32 = jnp.float32
BF16 = jnp.bfloat16

EPS = 1e-6
LANES = 128
MXU_COLS = 256
V7X_VMEM_LIMIT = 56 * 1024 * 1024

A_DILATIONS = (1, 4, 16)
A_HEADS = 8
A_HEAD_DIM = 128
A_WIDTH = A_HEADS * A_HEAD_DIM
A_ROT_DIM = A_HEAD_DIM // 4
A_ROPE_THETA = 500000.0
A_BLOCK = 128
B_CONV_WIDTH = 31
B_HALO = 32
C_QK_DIM = 256
C_HEADS = 8
C_V_DIM = 512
C_CHUNK = 128
C_ROPE_THETA = 10000.0


def _params(*sem):
    return pltpu.CompilerParams(dimension_semantics=sem, vmem_limit_bytes=V7X_VMEM_LIMIT)


def _resident(shape, index_map):
    return pl.BlockSpec(shape, index_map, pipeline_mode=pl.Buffered(1))


def _rms(x, w):
    y = x * lax.rsqrt(jnp.mean(x * x, axis=-1, keepdims=True) + EPS)
    return y * w


def _silu(x):
    return x * jax.nn.sigmoid(x)


def _perm_geometry(m, tm, dil):
    tile = A_BLOCK * dil
    big = max(tm, tile)
    n_sub, n_part, rows = big // tile, big // tm, min(tm, tile) // dil
    return (m // big, n_sub, dil, n_part, rows), (None, n_sub, dil, None, rows), n_part


def _perm_outputs(m, c, dtype, tm, dils, step):
    shapes = [jax.ShapeDtypeStruct((m, c), dtype)]
    specs = [pl.BlockSpec((tm, c), lambda *g: (step(*g), 0))]
    for dil in dils:
        shape, block, n_part = _perm_geometry(m, tm, dil)
        shapes.append(jax.ShapeDtypeStruct(shape + (c,), dtype))
        specs.append(pl.BlockSpec(block + (c,),
                                  lambda *g, n_part=n_part: (step(*g) // n_part, 0, 0, step(*g) % n_part, 0, 0)))
    return shapes, specs


def _write_perms(v, refs, slabs, dils):
    dtype = refs[0].dtype
    refs[0][...] = v.astype(dtype)
    if not dils:
        return
    tm = v.shape[0]
    n_slab = slabs[0].shape[0]
    for c in range(n_slab):
        slabs[0][c] = v[:, c * LANES:(c + 1) * LANES]
    span_prev, dil_prev = tm, 1
    for k, (ref, dil) in enumerate(zip(refs[1:], dils)):
        src, dst = slabs[k % 2], slabs[(k + 1) % 2]
        n_sub, _, rows, _ = ref.shape
        span = tm // n_sub
        step = dil // dil_prev
        pieces = span // span_prev if dil_prev > 1 else 1
        span_src = span // pieces
        rows_src = span_src // dil_prev
        for c in range(n_slab):
            sl = slice(c * LANES, (c + 1) * LANES)
            for s in range(n_sub):
                for r in range(dil):
                    r_prev, r_new = r % dil_prev, r // dil_prev
                    for p in range(pieces):
                        base = (s * pieces + p) * span_src + r_prev * rows_src + r_new
                        n = rows_src // step
                        t = src[c, pl.ds(base, n, stride=step), :]
                        ref[s, r, p * n:(p + 1) * n, sl] = t.astype(dtype)
                        if k + 1 < len(dils):
                            dst[c, s * span + r * rows + p * n:s * span + r * rows + (p + 1) * n, :] = t
        span_prev, dil_prev = span, dil


def _slab_scratch(d, tm, dils):
    return [pltpu.VMEM((d // LANES, tm, LANES), F32)] * min(len(dils), 2)


def _rmsnorm_kernel(x_ref, w_ref, *rest, dils):
    n_out = 1 + len(dils)
    _write_perms(_rms(x_ref[...], w_ref[...]), rest[:n_out], rest[n_out:], dils)


def _rmsnorm(x, w, dils, tm=512):
    m, d = x.shape
    shapes, specs = _perm_outputs(m, d, BF16, tm, dils, lambda i: i)
    return pl.pallas_call(
        functools.partial(_rmsnorm_kernel, dils=dils),
        out_shape=shapes,
        grid=(m // tm,),
        in_specs=[pl.BlockSpec((tm, d), lambda i: (i, 0)), _resident((1, d), lambda i: (0, 0))],
        out_specs=specs,
        scratch_shapes=_slab_scratch(d, tm, dils),
        compiler_params=_params("parallel"),
        name="rmsnorm_in",
    )(x, w.reshape(1, d))


def _rope_a_kernel(pos_ref, f_ref, *rest, dils):
    n_out = 3 * (1 + len(dils))
    outs, slabs = rest[:n_out], rest[n_out:]
    ang = pos_ref[...].astype(F32) * f_ref[...]
    sin = jnp.sin(ang)
    lane = lax.broadcasted_iota(jnp.int32, ang.shape, 1)
    half = A_ROT_DIM // 2
    tables = (jnp.cos(ang),
              jnp.where(lane < half, -sin, 0.0),
              jnp.where((lane >= half) & (lane < A_ROT_DIM), sin, 0.0))
    n = 1 + len(dils)
    for k, tab in enumerate(tables):
        _write_perms(tab, outs[k * n:(k + 1) * n], slabs, dils)


def _rope_c_kernel(pos_ref, f_ref, cos_ref, sin_ref):
    ang = pos_ref[...].astype(F32) * f_ref[...]
    cos_ref[...] = jnp.cos(ang)
    sin_ref[...] = jnp.sin(ang)


def _rope_a_tables(pos, freqs, dils, tm=1024):
    m = pos.shape[0]
    shapes, specs = _perm_outputs(m, LANES, F32, tm, dils, lambda i: i)
    outs = pl.pallas_call(
        functools.partial(_rope_a_kernel, dils=dils),
        out_shape=shapes * 3,
        grid=(m // tm,),
        in_specs=[pl.BlockSpec((tm, 1), lambda i: (i, 0)), _resident((1, LANES), lambda i: (0, 0))],
        out_specs=specs * 3,
        scratch_shapes=_slab_scratch(LANES, tm, dils),
        compiler_params=_params("parallel"),
        name="rope_a",
    )(pos, freqs)
    n = 1 + len(dils)
    return [tuple(outs[k * n + o].reshape(m, LANES) for k in range(3)) for o in range(n)]


def _rope_c_tables(pos, freqs, tm=1024):
    m = pos.shape[0]
    row = pl.BlockSpec((tm, LANES), lambda i: (i, 0))
    return pl.pallas_call(
        _rope_c_kernel,
        out_shape=(jax.ShapeDtypeStruct((m, LANES), F32),) * 2,
        grid=(m // tm,),
        in_specs=[pl.BlockSpec((tm, 1), lambda i: (i, 0)), _resident((1, LANES), lambda i: (0, 0))],
        out_specs=(row, row),
        compiler_params=_params("parallel"),
        name="rope_c",
    )(pos, freqs)


def _rope_freqs():
    half_a = A_ROT_DIM // 2
    inv_a = A_ROPE_THETA ** (-jnp.arange(0, A_ROT_DIM, 2, dtype=F32) / A_ROT_DIM)
    fa = jnp.concatenate([inv_a, inv_a, jnp.zeros((LANES - 2 * half_a,), F32)]).reshape(1, LANES)
    fc = (C_ROPE_THETA ** (-jnp.arange(0, C_QK_DIM, 2, dtype=F32) / C_QK_DIM)).reshape(1, LANES)
    return fa, fc


def _rotary_a(t, cos, slo, shi):
    return (t * cos + pltpu.roll(t, LANES - A_ROT_DIM // 2, 1) * slo + pltpu.roll(t, A_ROT_DIM // 2, 1) * shi)


def _proj_a_kernel(h_ref, wqkv_ref, *rest, has_gate):
    if has_gate:
        wg_ref, cos_ref, slo_ref, shi_ref, o_ref = rest
    else:
        wg_ref, (cos_ref, slo_ref, shi_ref, o_ref) = None, rest
    h = h_ref[...]
    cos, slo, shi = cos_ref[...], slo_ref[...], shi_ref[...]
    scale = A_HEAD_DIM ** -0.5
    q_tabs = (cos * scale, slo * scale, shi * scale)
    for c in range(o_ref.shape[1] // MXU_COLS):
        tile, off = divmod(c * MXU_COLS, A_WIDTH)
        w_chunk = (wg_ref[:, off:off + MXU_COLS] if tile == 3
                   else wqkv_ref[:, c * MXU_COLS:(c + 1) * MXU_COLS])
        acc = jnp.dot(h, w_chunk, preferred_element_type=F32)
        for k in range(MXU_COLS // A_HEAD_DIM):
            t = acc[:, k * A_HEAD_DIM:(k + 1) * A_HEAD_DIM]
            if tile == 0:
                t = _rotary_a(t, *q_tabs)
            elif tile == 1:
                t = _rotary_a(t, cos, slo, shi)
            elif tile == 3:
                t = _silu(t)
            lo = c * MXU_COLS + k * A_HEAD_DIM
            o_ref[:, lo:lo + A_HEAD_DIM] = t.astype(BF16)


def _proj_a(h, w, tabs, g, tm=1024):
    m, d = h.shape
    has_gate = g == 0
    n_out = (4 if has_gate else 3) * A_WIDTH
    gate_tile = w.shape[1] // A_WIDTH - 1
    tab = pl.BlockSpec((tm, LANES), lambda i: (i, 0))
    w_specs = [_resident((d, 3 * A_WIDTH), lambda i: (0, g))]
    w_args = [w]
    if has_gate:
        w_specs.append(_resident((d, A_WIDTH), lambda i: (0, gate_tile)))
        w_args.append(w)
    return pl.pallas_call(
        functools.partial(_proj_a_kernel, has_gate=has_gate),
        out_shape=jax.ShapeDtypeStruct((m, n_out), BF16),
        grid=(m // tm,),
        in_specs=[pl.BlockSpec((tm, d), lambda i: (i, 0))] + w_specs + [tab, tab, tab],
        out_specs=pl.BlockSpec((tm, n_out), lambda i: (i, 0)),
        compiler_params=_params("parallel"),
        name=f"proj_a{g}",
    )(h, *w_args, *tabs)


def _proj_c_qk_kernel(h_ref, w_ref, cos_ref, sin_ref, o_ref):
    cos, sin = cos_ref[...], sin_ref[...]
    scale = C_QK_DIM ** -0.5
    k_tabs = (cos * scale, sin * scale)
    h = h_ref[...]
    half = C_QK_DIM // 2
    for c in range(o_ref.shape[1] // C_QK_DIM):
        acc = jnp.dot(h, w_ref[:, c * C_QK_DIM:(c + 1) * C_QK_DIM], preferred_element_type=F32)
        t1, t2 = acc[:, :half], acc[:, half:]
        cs, sn = (cos, sin) if c < C_HEADS else k_tabs
        o_ref[:, c * C_QK_DIM:c * C_QK_DIM + half] = (t1 * cs - t2 * sn).astype(BF16)
        o_ref[:, c * C_QK_DIM + half:(c + 1) * C_QK_DIM] = (t2 * cs + t1 * sn).astype(BF16)


def _proj_glu_kernel(h_ref, wa_ref, wb_ref, u_ref):
    h = h_ref[...]
    for c in range(u_ref.shape[1] // MXU_COLS):
        sl = slice(c * MXU_COLS, (c + 1) * MXU_COLS)
        a = jnp.dot(h, wa_ref[:, sl], preferred_element_type=F32)
        b = jnp.dot(h, wb_ref[:, sl], preferred_element_type=F32)
        u_ref[:, sl] = (a * jax.nn.sigmoid(b)).astype(BF16)


def _proj_act_kernel(h_ref, w_ref, o_ref, *, act):
    h = h_ref[...]
    for c in range(o_ref.shape[1] // MXU_COLS):
        sl = slice(c * MXU_COLS, (c + 1) * MXU_COLS)
        acc = jnp.dot(h, w_ref[:, sl], preferred_element_type=F32)
        o_ref[:, sl] = (_silu(acc) if act else acc).astype(BF16)


def _proj_cols(body, h, ws, first_tiles, tn, extra=(), name="proj", tm=1024):
    m, d = h.shape
    tab = pl.BlockSpec((tm, LANES), lambda i: (i, 0))
    w_specs = [_resident((d, tn), lambda i, f=f: (0, f)) for f in first_tiles]
    return pl.pallas_call(
        body,
        out_shape=jax.ShapeDtypeStruct((m, tn), BF16),
        grid=(m // tm,),
        in_specs=[pl.BlockSpec((tm, d), lambda i: (i, 0))] + w_specs + [tab] * len(extra),
        out_specs=pl.BlockSpec((tm, tn), lambda i: (i, 0)),
        compiler_params=_params("parallel"),
        name=name,
    )(h, *ws, *extra)


def _proj_c(h, w, tabs):
    tn = 2 * C_HEADS * C_QK_DIM
    qk = _proj_cols(_proj_c_qk_kernel, h, [w], [0], tn, tabs, "proj_c_qk")
    v = _proj_cols(functools.partial(_proj_act_kernel, act=False), h, [w], [1], tn, name="proj_c_v")
    gate = _proj_cols(functools.partial(_proj_act_kernel, act=True), h, [w], [2], tn, name="proj_c_gate")
    return qk, v, gate


def _proj_b_glu(h, w):
    return _proj_cols(_proj_glu_kernel, h, [w, w], [0, 1], w.shape[1] // 3, name="proj_b_glu")


def _attn_kernel(q_ref, k_ref, kp_ref, v_ref, vp_ref, o_ref, lse_ref, *, nb):
    first = pl.program_id(2) == 0
    row = lax.broadcasted_iota(jnp.int32, (A_BLOCK, 2 * A_BLOCK), 0)
    col = lax.broadcasted_iota(jnp.int32, (A_BLOCK, 2 * A_BLOCK), 1)
    in_prev = (col < A_BLOCK) & (col >= row)
    in_cur = (col >= A_BLOCK) & (col - A_BLOCK <= row)
    bias_inner = jnp.where(in_prev | in_cur, 0.0, -jnp.inf).astype(F32)
    bias_first = jnp.where((in_prev & jnp.logical_not(first)) | in_cur, 0.0, -jnp.inf).astype(F32)
    nt = (((1,), (1,)), ((), ()))
    heads = [slice(hd * A_HEAD_DIM, (hd + 1) * A_HEAD_DIM) for hd in range(A_HEADS)]
    items = [(g, hd) for g in range(nb) for hd in range(A_HEADS)]

    def keys(ref, prev_ref, g, sl):
        prev = prev_ref[:, sl] if g == 0 else ref[g - 1, :, sl]
        return jnp.concatenate([prev, ref[g, :, sl]], axis=0)

    scores = [lax.dot_general(q_ref[g, :, heads[hd]], keys(k_ref, kp_ref, g, heads[hd]), nt,
                              preferred_element_type=F32) + (bias_first if g == 0 else bias_inner)
              for g, hd in items]
    maxes = [jnp.max(s, axis=1, keepdims=True) for s in scores]
    probs = [jnp.exp(s - mx) for s, mx in zip(scores, maxes)]
    sums = [jnp.sum(p, axis=1, keepdims=True) for p in probs]
    outs = [jnp.dot(p.astype(BF16), keys(v_ref, vp_ref, g, heads[hd]), preferred_element_type=F32)
            for (g, hd), p in zip(items, probs)]
    lse_ref[...] = jnp.zeros(lse_ref.shape, F32)
    for n, (g, hd) in enumerate(items):
        o_ref[g, :, heads[hd]] = (outs[n] / sums[n]).astype(BF16)
        lse_ref[g, :, hd:hd + 1] = maxes[n] + jnp.log(sums[n])


def _attention_group(proj, batch, seq, dil, nb=4):
    m = batch * seq
    nblk = seq // (A_BLOCK * dil)
    nb = math.gcd(nb, nblk)
    steps = nblk // nb
    view = proj.reshape(batch * nblk, dil, A_BLOCK, proj.shape[1])

    def cur(tile):
        return pl.BlockSpec((nb, None, A_BLOCK, A_WIDTH), lambda b, r, t: (b * steps + t, r, 0, tile))

    def prev(tile):
        return pl.BlockSpec((None, None, A_BLOCK, A_WIDTH),
                            lambda b, r, t: (b * nblk + jnp.maximum(t * nb - 1, 0), r, 0, tile))

    o, lse = pl.pallas_call(
        functools.partial(_attn_kernel, nb=nb),
        out_shape=(jax.ShapeDtypeStruct((batch * nblk, dil, A_BLOCK, A_WIDTH), BF16),
                   jax.ShapeDtypeStruct((batch * nblk, dil, A_BLOCK, LANES), F32)),
        grid=(batch, dil, steps),
        in_specs=[cur(0), cur(1), prev(1), cur(2), prev(2)],
        out_specs=(pl.BlockSpec((nb, None, A_BLOCK, A_WIDTH), lambda b, r, t: (b * steps + t, r, 0, 0)),
                   pl.BlockSpec((nb, None, A_BLOCK, LANES), lambda b, r, t: (b * steps + t, r, 0, 0))),
        compiler_params=_params("parallel", "parallel", "arbitrary"),
        name=f"attn_d{dil}",
    )(view, view, view, view, view)
    return o.reshape(m, A_WIDTH), lse.reshape(m, LANES)


def _finish(y, x_ref, nw_ref, out_refs, slabs, last, dils):
    x_new = x_ref[...] + y
    h = _rms(x_new, nw_ref[...])
    if last:
        out_refs[0][...] = h
    else:
        out_refs[0][...] = x_new
        _write_perms(h, out_refs[1:], slabs, dils)


def _tail_outputs(m, d, tm, last, dils, step):
    x_shape = jax.ShapeDtypeStruct((m, d), F32)
    x_spec = pl.BlockSpec((tm, d), lambda *g: (step(*g), 0))
    if last:
        return [x_shape], [x_spec]
    shapes, specs = _perm_outputs(m, d, BF16, tm, dils, step)
    return [x_shape] + shapes, [x_spec] + specs


def _split_tail(rest, n_scratch, last, dils):
    n_out = 1 if last else 2 + len(dils)
    return rest[:n_out], rest[n_out:len(rest) - n_scratch], rest[len(rest) - n_scratch:]


def _gather_residue_major(ref, slab_ref, tmp_ref=None, inner=1):
    n_sub, dil, rows, c = ref.shape
    span = dil * rows
    outer = dil // inner
    for s in range(n_sub):
        for r in range(dil):
            v = ref[s, r].astype(F32)
            for k in range(c // LANES):
                piece = v[:, k * LANES:(k + 1) * LANES]
                if inner == 1:
                    slab_ref[k, pl.ds(s * span + r, rows, stride=dil), :] = piece
                else:
                    base = s * span + (r % inner) * (span // inner) + r // inner
                    tmp_ref[k, pl.ds(base, rows, stride=outer), :] = piece
    if inner > 1:
        for k in range(c // LANES):
            for s in range(n_sub):
                for r in range(inner):
                    lo = s * span + r * (span // inner)
                    slab_ref[k, pl.ds(s * span + r, span // inner, stride=inner), :] = tmp_ref[k, lo:lo + span // inner, :]


def _out_a_kernel(o1_ref, o2_ref, o3_ref, l1_ref, l2_ref, l3_ref, g_ref, x_ref, w_ref, nw_ref, *rest, last, dils):
    out_refs, slabs, (y_ref, y_new, acc_ref, os2, os3, o_tmp, ls2, ls3) = _split_tail(rest, 8, last, dils)

    @pl.when(pl.program_id(0) == 0)
    def _():
        y_ref[...] = jnp.zeros(y_ref.shape, BF16)

    inner = A_DILATIONS[1]
    _gather_residue_major(o2_ref, os2)
    _gather_residue_major(o3_ref, os3, o_tmp, inner)
    _gather_residue_major(l2_ref, ls2)
    _gather_residue_major(l3_ref, ls3, o_tmp, inner)
    l1, l2, l3 = l1_ref[:, :A_HEADS], ls2[0, :, :A_HEADS], ls3[0, :, :A_HEADS]
    mx = jnp.maximum(jnp.maximum(l1, l2), l3)
    e1, e2, e3 = jnp.exp(l1 - mx), jnp.exp(l2 - mx), jnp.exp(l3 - mx)
    den = e1 + e2 + e3
    a1, a2, a3 = e1 / den, e2 / den, e3 / den
    y_prev = y_ref[...]
    zero = jnp.zeros((1, LANES), F32)
    n_chunks = acc_ref.shape[1] // MXU_COLS
    for hd in range(A_HEADS):
        for cc in range(hd * n_chunks // A_HEADS, (hd + 1) * n_chunks // A_HEADS):
            cols = slice(cc * MXU_COLS, (cc + 1) * MXU_COLS)
            proj = jnp.dot(y_prev, w_ref[:, cols], preferred_element_type=F32)
            acc_ref[:, cols] = proj
        sl = slice(hd * A_HEAD_DIM, (hd + 1) * A_HEAD_DIM)
        c = slice(hd, hd + 1)
        o = a1[:, c] * o1_ref[:, sl].astype(F32) + a2[:, c] * os2[hd] + a3[:, c] * os3[hd] + zero
        y_new[:, sl] = (o * g_ref[:, sl].astype(F32)).astype(BF16)
        zero = _order_after(proj)
    _finish(acc_ref[...], x_ref, nw_ref, out_refs, slabs, last, dils)
    y_ref[...] = y_new[...]


def _residue_major_input(a, tm, dil, tile):
    m, c = a.shape
    shape, block, n_part = _perm_geometry(m, tm, dil)
    spec = pl.BlockSpec(block + (c,), lambda s: (tile(s) // n_part, 0, 0, tile(s) % n_part, 0, 0))
    return a.reshape(shape + (c,)), spec


def _out_a(os_, lses, proj0, x, w, nw, last, dils, tm=512):
    m, d = x.shape
    n_tiles = m // tm
    cur = lambda s: jnp.minimum(s, n_tiles - 1)
    lag = lambda s: jnp.maximum(s - 1, 0)
    row = lambda s: (cur(s), 0)
    o2, o2_spec = _residue_major_input(os_[1], tm, A_DILATIONS[1], cur)
    o3, o3_spec = _residue_major_input(os_[2], tm, A_DILATIONS[2], cur)
    l2, l2_spec = _residue_major_input(lses[1], tm, A_DILATIONS[1], cur)
    l3, l3_spec = _residue_major_input(lses[2], tm, A_DILATIONS[2], cur)
    shapes, specs = _tail_outputs(m, d, tm, last, dils, lag)
    o_slab = pltpu.VMEM((A_HEADS, tm, LANES), F32)
    l_slab = pltpu.VMEM((1, tm, LANES), F32)
    y_buf = pltpu.VMEM((tm, A_WIDTH), BF16)
    return pl.pallas_call(
        functools.partial(_out_a_kernel, last=last, dils=dils),
        out_shape=shapes,
        grid=(n_tiles + 1,),
        in_specs=[pl.BlockSpec((tm, A_WIDTH), row), o2_spec, o3_spec,
                  pl.BlockSpec((tm, LANES), row), l2_spec, l3_spec,
                  pl.BlockSpec((tm, A_WIDTH), lambda s: (cur(s), 3)),
                  pl.BlockSpec((tm, d), lambda s: (lag(s), 0)),
                  _resident((A_WIDTH, d), lambda s: (0, 0)), _resident((1, d), lambda s: (0, 0))],
        out_specs=specs,
        scratch_shapes=([] if last else _slab_scratch(d, tm, dils))
        + [y_buf, y_buf, pltpu.VMEM((tm, d), F32), o_slab, o_slab, o_slab, l_slab, l_slab],
        compiler_params=_params("arbitrary"),
        name="out_a",
    )(os_[0], o2, o3, lses[0], l2, l3, proj0, x, w, nw.reshape(1, d))


def _order_after(*values):
    bits = None
    for v in values:
        b = pltpu.bitcast(v[v.shape[0] - 8:, :LANES], jnp.uint32)
        bits = b if bits is None else bits | b
    sixteen = jnp.uint32(16)
    cleared = lax.shift_right_logical(lax.shift_right_logical(bits, sixteen), sixteen)
    return pltpu.bitcast(cleared, F32)[0:1, :]


def _out_b_kernel(u_ref, up_ref, h_ref, x_ref, wg_ref, cw_ref, cb_ref, lw_ref, lb_ref, w_ref, nw_ref, *rest,
                  last, tm, rc, tiles_per_seq, n_tiles):
    out_refs, (ubuf, cbuf, gbuf, y_ref, acc_ref) = rest[:-5], rest[-5:]
    s = pl.program_id(0)

    @pl.when(s == 0)
    def _():
        y_ref[...] = jnp.zeros(y_ref.shape, BF16)

    tile = jnp.minimum(s, n_tiles - 1)
    n_slab = ubuf.shape[0]
    halo = jnp.where(tile % tiles_per_seq > 0, up_ref[...].astype(F32), 0.0)
    for c in range(n_slab):
        sl = slice(c * LANES, (c + 1) * LANES)
        ubuf[c, 0:B_HALO, :] = halo[:, sl]
        ubuf[c, B_HALO:, :] = u_ref[:, sl].astype(F32)
    first = B_HALO - (B_CONV_WIDTH - 1)
    h = h_ref[...]
    y_prev = y_ref[...]
    per = MXU_COLS // LANES
    zero = jnp.zeros((1, LANES), F32)
    for cc in range(n_slab // per):
        cols = slice(cc * MXU_COLS, (cc + 1) * MXU_COLS)
        gate = jnp.dot(h, wg_ref[:, cols], preferred_element_type=F32)
        proj = jnp.dot(y_prev, w_ref[:, cols], preferred_element_type=F32)
        gbuf[:, cols] = _silu(gate).astype(BF16)
        acc_ref[:, cols] = proj
        for c in range(cc * per, (cc + 1) * per):
            sl = slice(c * LANES, (c + 1) * LANES)
            taps = [cw_ref[j:j + 1, sl] for j in range(B_CONV_WIDTH)]
            for r0 in range(0, tm, rc):
                acc = jnp.broadcast_to(cb_ref[:, sl] + zero, (rc, LANES))
                for j in range(B_CONV_WIDTH):
                    acc = acc + ubuf[c, r0 + first + j:r0 + first + j + rc, :] * taps[j]
                cbuf[r0:r0 + rc, sl] = acc
        zero = _order_after(gate, proj)
    x_new = x_ref[...] + acc_ref[...]
    h_new = _rms(x_new, nw_ref[...])
    if last:
        out_refs[0][...] = h_new
    else:
        out_refs[0][...] = x_new
        out_refs[1][...] = h_new.astype(BF16)
    cv = cbuf[...]
    mu = jnp.mean(cv, axis=-1, keepdims=True)
    cen = cv - mu
    var = jnp.mean(cen * cen, axis=-1, keepdims=True)
    cn = cen * lax.rsqrt(var + EPS) * lw_ref[...] + lb_ref[...]
    y_ref[...] = (_silu(cn) * gbuf[...].astype(F32)).astype(BF16)


def _out_b(u, h, x, w_in, conv_w, conv_b, ln_w, ln_b, w, nw, seq, last, tm=256, rc=64):
    m, d = x.shape
    width = u.shape[1]
    tiles_per_seq = seq // tm
    n_tiles = m // tm
    hb = tm // B_HALO
    cur = lambda s: (jnp.minimum(s, n_tiles - 1), 0)
    lag = lambda s: (jnp.maximum(s - 1, 0), 0)
    vec = lambda a: a.reshape(1, -1)
    const = lambda s: (0, 0)
    shapes, specs = _tail_outputs(m, d, tm, last, (), lambda s: jnp.maximum(s - 1, 0))
    return pl.pallas_call(
        functools.partial(_out_b_kernel, last=last, tm=tm, rc=rc, tiles_per_seq=tiles_per_seq, n_tiles=n_tiles),
        out_shape=shapes,
        grid=(n_tiles + 1,),
        in_specs=[pl.BlockSpec((tm, width), cur),
                  pl.BlockSpec((B_HALO, width), lambda s: (jnp.maximum(jnp.minimum(s, n_tiles - 1) * hb - 1, 0), 0)),
                  pl.BlockSpec((tm, d), cur),
                  pl.BlockSpec((tm, d), lag),
                  _resident((d, width), lambda s: (0, 2)),
                  _resident((B_CONV_WIDTH, width), const), _resident((1, width), const),
                  _resident((1, width), const), _resident((1, width), const),
                  _resident((width, d), const), _resident((1, d), const)],
        out_specs=specs,
        scratch_shapes=[pltpu.VMEM((width // LANES, tm + B_HALO, LANES), F32),
                        pltpu.VMEM((tm, width), F32), pltpu.VMEM((tm, width), BF16),
                        pltpu.VMEM((tm, width), BF16), pltpu.VMEM((tm, d), F32)],
        compiler_params=_params("arbitrary"),
        name="out_b",
    )(u, u, h, x, w_in, conv_w, vec(conv_b), vec(ln_w), vec(ln_b), w, vec(nw))


def _retention_kernel(q_ref, k_ref, v_ref, g_ref, dm_ref, qd_ref, kd_ref, cd_ref, y_ref, state, *, tc, ch):
    @pl.when(pl.program_id(2) == 0)
    def _():
        state[...] = jnp.zeros(state.shape, F32)

    nt = (((1,), (1,)), ((), ()))
    tn = (((0,), (0,)), ((), ()))
    dm, qd, kd, cd = dm_ref[...], qd_ref[...], kd_ref[...], cd_ref[...]
    n = tc // ch
    rows = [slice(c * ch, (c + 1) * ch) for c in range(n)]
    qs = [q_ref[rs, :] for rs in rows]
    ks = [k_ref[rs, :] for rs in rows]
    vs = [v_ref[rs, :] for rs in rows]
    inner = [(lax.dot_general(q, k, nt, preferred_element_type=F32) * dm).astype(BF16) for q, k in zip(qs, ks)]
    kdec = [(k.astype(F32) * kd).astype(BF16) for k in ks]
    intra = [jnp.dot(a, v, preferred_element_type=F32) for a, v in zip(inner, vs)]
    update = [lax.dot_general(a, v, tn, preferred_element_type=F32) for a, v in zip(kdec, vs)]
    st = state[...]
    for c in range(n):
        cross = jnp.dot(qs[c], st.astype(BF16), preferred_element_type=F32)
        o = intra[c] + cross * qd
        st = st * cd + update[c]
        if c == n - 1:
            state[...] = st
        mu = jnp.mean(o, axis=-1, keepdims=True)
        cen = o - mu
        var = jnp.mean(cen * cen, axis=-1, keepdims=True)
        on = cen * lax.rsqrt(var + EPS)
        y_ref[rows[c], :] = (on * g_ref[rows[c], :].astype(F32)).astype(BF16)


def _retention(qk, v, gate, batch, seq, tc=1024, ch=256):
    m = batch * seq
    nt = seq // tc
    vw = C_HEADS * C_V_DIM

    gammas = 1.0 - jnp.exp(jnp.linspace(math.log(1.0 / 32), math.log(1.0 / 512), C_HEADS, dtype=F32))
    log_g = jnp.log(gammas)
    idx = jnp.arange(ch, dtype=F32)
    diff = idx[:, None] - idx[None, :]
    decay_mask = jnp.where(diff >= 0, jnp.exp(jnp.maximum(diff, 0.0)[None] * log_g[:, None, None]), 0.0)
    q_decay = jnp.exp((idx[None] + 1.0) * log_g[:, None])
    k_decay = jnp.exp((ch - 1.0 - idx[None]) * log_g[:, None])
    chunk_decay = jnp.exp(ch * log_g)
    qd = jnp.broadcast_to(q_decay[:, :, None], (C_HEADS, ch, C_V_DIM))
    kd = jnp.broadcast_to(k_decay[:, :, None], (C_HEADS, ch, C_QK_DIM))
    cd = jnp.broadcast_to(chunk_decay[:, None, None], (C_HEADS, 1, C_V_DIM))

    head = lambda b, h, i: (h, 0, 0)
    rows = lambda off: (lambda b, h, i: (b * nt + i, off + h))
    return pl.pallas_call(
        functools.partial(_retention_kernel, tc=tc, ch=ch),
        out_shape=jax.ShapeDtypeStruct((m, vw), BF16),
        grid=(batch, C_HEADS, nt),
        in_specs=[pl.BlockSpec((tc, C_QK_DIM), rows(0)),
                  pl.BlockSpec((tc, C_QK_DIM), rows(C_HEADS)),
                  pl.BlockSpec((tc, C_V_DIM), rows(0)),
                  pl.BlockSpec((tc, C_V_DIM), rows(0)),
                  pl.BlockSpec((None, ch, ch), head),
                  pl.BlockSpec((None, ch, C_V_DIM), head),
                  pl.BlockSpec((None, ch, C_QK_DIM), head),
                  pl.BlockSpec((None, 1, C_V_DIM), head)],
        out_specs=pl.BlockSpec((tc, C_V_DIM), rows(0)),
        scratch_shapes=[pltpu.VMEM((C_QK_DIM, C_V_DIM), F32)],
        compiler_params=_params("parallel", "parallel", "arbitrary"),
        name="retention",
    )(qk, qk, v, gate, decay_mask, qd, kd, cd)


def _out_c_kernel(y_ref, x_ref, w_ref, nw_ref, *rest, last, dils):
    out_refs, slabs, _ = _split_tail(rest, 0, last, dils)
    y = jnp.dot(y_ref[...], w_ref[...], preferred_element_type=F32)
    _finish(y, x_ref, nw_ref, out_refs, slabs, last, dils)


def _out_c(y, x, w, nw, last, dils, tm=256):
    m, d = x.shape
    kdim = y.shape[1]
    row = lambda i: (i, 0)
    shapes, specs = _tail_outputs(m, d, tm, last, dils, lambda i: i)
    return pl.pallas_call(
        functools.partial(_out_c_kernel, last=last, dils=dils),
        out_shape=shapes,
        grid=(m // tm,),
        in_specs=[pl.BlockSpec((tm, kdim), row), pl.BlockSpec((tm, d), row),
                  _resident((kdim, d), lambda i: (0, 0)), _resident((1, d), lambda i: (0, 0))],
        out_specs=specs,
        scratch_shapes=[] if last else _slab_scratch(d, tm, dils),
        compiler_params=_params("parallel"),
        name="out_c",
    )(y, x, w, nw.reshape(1, d))


def kernel(x, positions, norm_w, final_norm_w, a_w_in, a_w_out, b_w_in, b_conv_w, b_conv_b, b_ln_w, b_ln_b,
           b_w_out, c_w_in, c_w_out):
    batch, seq, d = x.shape
    depth = norm_w.shape[0]
    m = batch * seq
    xs = x.reshape(m, d)
    perm_dils = tuple(dl for dl in A_DILATIONS if dl > 1)

    def next_dils(i):
        return perm_dils if (i + 1 < depth and (i + 1) % 3 == 0) else ()

    fa, fc = _rope_freqs()
    pos = positions.reshape(m, 1)
    tabs_a = _rope_a_tables(pos, fa, perm_dils)
    tabs_c = _rope_c_tables(pos, fc)

    hs = _rmsnorm(xs, norm_w[0], perm_dils)
    for i in range(depth):
        kind, j = i % 3, i // 3
        last = i == depth - 1
        nw = final_norm_w if last else norm_w[i + 1]
        dils = next_dils(i)
        if kind == 0:
            w_in = a_w_in[j].astype(BF16)
            os_, lses, proj0 = [], [], None
            for g, dl in enumerate(A_DILATIONS):
                h_g = hs[g].reshape(m, d)
                proj = _proj_a(h_g, w_in, tabs_a[g], g)
                proj0 = proj if g == 0 else proj0
                o, lse = _attention_group(proj, batch, seq, dl)
                os_.append(o)
                lses.append(lse)
            outs = _out_a(os_, lses, proj0, xs, a_w_out[j].astype(BF16), nw, last, dils)
        elif kind == 1:
            assert not dils
            w_in = b_w_in[j].astype(BF16)
            u = _proj_b_glu(hs[0], w_in)
            outs = _out_b(u, hs[0], xs, w_in, b_conv_w[j], b_conv_b[j], b_ln_w[j], b_ln_b[j],
                          b_w_out[j].astype(BF16), nw, seq, last)
        else:
            qk, v, gate = _proj_c(hs[0], c_w_in[j].astype(BF16), tabs_c)
            y = _retention(qk, v, gate, batch, seq)
            outs = _out_c(y, xs, c_w_out[j].astype(BF16), nw, last, dils)
        if last:
            return outs[0].reshape(batch, seq, d)
        xs, hs = outs[0], outs[1:]
```

```python
import functools
import math

import jax
import jax.numpy as jnp
from jax import lax
from jax.experimental import pallas as pl
from jax.experimental.pallas import tpu as pltpu

F32 = jnp.float32
BF16 = jnp.bfloat16

EPS = 1e-6
LANES = 128
MXU_COLS = 256
V7X_VMEM_LIMIT = 56 * 1024 * 1024

A_DILATIONS = (1, 4, 16)
A_HEADS = 8
A_HEAD_DIM = 128
A_WIDTH = A_HEADS * A_HEAD_DIM
A_ROT_DIM = A_HEAD_DIM // 4
A_ROPE_THETA = 500000.0
A_BLOCK = 128
B_CONV_WIDTH = 31
B_HALO = 32
C_QK_DIM = 256
C_HEADS = 8
C_V_DIM = 512
C_CHUNK = 128
C_ROPE_THETA = 10000.0


def _params(*sem):
    return pltpu.CompilerParams(dimension_semantics=sem, vmem_limit_bytes=V7X_VMEM_LIMIT)


def _resident(shape, index_map):
    return pl.BlockSpec(shape, index_map, pipeline_mode=pl.Buffered(1))


def _rms(x, w):
    y = x * lax.rsqrt(jnp.mean(x * x, axis=-1, keepdims=True) + EPS)
    return y * w


def _silu(x):
    return x * jax.nn.sigmoid(x)


def _perm_geometry(m, tm, dil):
    tile = A_BLOCK * dil
    big = max(tm, tile)
    n_sub, n_part, rows = big // tile, big // tm, min(tm, tile) // dil
    return (m // big, n_sub, dil, n_part, rows), (None, n_sub, dil, None, rows), n_part


def _perm_outputs(m, c, dtype, tm, dils, step):
    shapes = [jax.ShapeDtypeStruct((m, c), dtype)]
    specs = [pl.BlockSpec((tm, c), lambda *g: (step(*g), 0))]
    for dil in dils:
        shape, block, n_part = _perm_geometry(m, tm, dil)
        shapes.append(jax.ShapeDtypeStruct(shape + (c,), dtype))
        specs.append(pl.BlockSpec(block + (c,),
                                  lambda *g, n_part=n_part: (step(*g) // n_part, 0, 0, step(*g) % n_part, 0, 0)))
    return shapes, specs


def _write_perms(v, refs, slabs, dils):
    dtype = refs[0].dtype
    refs[0][...] = v.astype(dtype)
    if not dils:
        return
    tm = v.shape[0]
    n_slab = slabs[0].shape[0]
    for c in range(n_slab):
        slabs[0][c] = v[:, c * LANES:(c + 1) * LANES]
    span_prev, dil_prev = tm, 1
    for k, (ref, dil) in enumerate(zip(refs[1:], dils)):
        src, dst = slabs[k % 2], slabs[(k + 1) % 2]
        n_sub, _, rows, _ = ref.shape
        span = tm // n_sub
        step = dil // dil_prev
        pieces = span // span_prev if dil_prev > 1 else 1
        span_src = span // pieces
        rows_src = span_src // dil_prev
        for c in range(n_slab):
            sl = slice(c * LANES, (c + 1) * LANES)
            for s in range(n_sub):
                for r in range(dil):
                    r_prev, r_new = r % dil_prev, r // dil_prev
                    for p in range(pieces):
                        base = (s * pieces + p) * span_src + r_prev * rows_src + r_new
                        n = rows_src // step
                        t = src[c, pl.ds(base, n, stride=step), :]
                        ref[s, r, p * n:(p + 1) * n, sl] = t.astype(dtype)
                        if k + 1 < len(dils):
                            dst[c, s * span + r * rows + p * n:s * span + r * rows + (p + 1) * n, :] = t
        span_prev, dil_prev = span, dil


def _slab_scratch(d, tm, dils):
    return [pltpu.VMEM((d // LANES, tm, LANES), F32)] * min(len(dils), 2)


def _rmsnorm_kernel(x_ref, w_ref, *rest, dils):
    n_out = 1 + len(dils)
    _write_perms(_rms(x_ref[...], w_ref[...]), rest[:n_out], rest[n_out:], dils)


def _rmsnorm(x, w, dils, tm=512):
    m, d = x.shape
    shapes, specs = _perm_outputs(m, d, BF16, tm, dils, lambda i: i)
    return pl.pallas_call(
        functools.partial(_rmsnorm_kernel, dils=dils),
        out_shape=shapes,
        grid=(m // tm,),
        in_specs=[pl.BlockSpec((tm, d), lambda i: (i, 0)), _resident((1, d), lambda i: (0, 0))],
        out_specs=specs,
        scratch_shapes=_slab_scratch(d, tm, dils),
        compiler_params=_params("parallel"),
        name="rmsnorm_in",
    )(x, w.reshape(1, d))


def _rope_a_kernel(pos_ref, f_ref, *rest, dils):
    n_out = 3 * (1 + len(dils))
    outs, slabs = rest[:n_out], rest[n_out:]
    ang = pos_ref[...].astype(F32) * f_ref[...]
    sin = jnp.sin(ang)
    lane = lax.broadcasted_iota(jnp.int32, ang.shape, 1)
    half = A_ROT_DIM // 2
    tables = (jnp.cos(ang),
              jnp.where(lane < half, -sin, 0.0),
              jnp.where((lane >= half) & (lane < A_ROT_DIM), sin, 0.0))
    n = 1 + len(dils)
    for k, tab in enumerate(tables):
        _write_perms(tab, outs[k * n:(k + 1) * n], slabs, dils)


def _rope_c_kernel(pos_ref, f_ref, cos_ref, sin_ref):
    ang = pos_ref[...].astype(F32) * f_ref[...]
    cos_ref[...] = jnp.cos(ang)
    sin_ref[...] = jnp.sin(ang)


def _rope_a_tables(pos, freqs, dils, tm=1024):
    m = pos.shape[0]
    shapes, specs = _perm_outputs(m, LANES, F32, tm, dils, lambda i: i)
    outs = pl.pallas_call(
        functools.partial(_rope_a_kernel, dils=dils),
        out_shape=shapes * 3,
        grid=(m // tm,),
        in_specs=[pl.BlockSpec((tm, 1), lambda i: (i, 0)), _resident((1, LANES), lambda i: (0, 0))],
        out_specs=specs * 3,
        scratch_shapes=_slab_scratch(LANES, tm, dils),
        compiler_params=_params("parallel"),
        name="rope_a",
    )(pos, freqs)
    n = 1 + len(dils)
    return [tuple(outs[k * n + o].reshape(m, LANES) for k in range(3)) for o in range(n)]


def _rope_c_tables(pos, freqs, tm=1024):
    m = pos.shape[0]
    row = pl.BlockSpec((tm, LANES), lambda i: (i, 0))
    return pl.pallas_call(
        _rope_c_kernel,
        out_shape=(jax.ShapeDtypeStruct((m, LANES), F32),) * 2,
        grid=(m // tm,),
        in_specs=[pl.BlockSpec((tm, 1), lambda i: (i, 0)), _resident((1, LANES), lambda i: (0, 0))],
        out_specs=(row, row),
        compiler_params=_params("parallel"),
        name="rope_c",
    )(pos, freqs)


def _rope_freqs():
    half_a = A_ROT_DIM // 2
    inv_a = A_ROPE_THETA ** (-jnp.arange(0, A_ROT_DIM, 2, dtype=F32) / A_ROT_DIM)
    fa = jnp.concatenate([inv_a, inv_a, jnp.zeros((LANES - 2 * half_a,), F32)]).reshape(1, LANES)
    fc = (C_ROPE_THETA ** (-jnp.arange(0, C_QK_DIM, 2, dtype=F32) / C_QK_DIM)).reshape(1, LANES)
    return fa, fc


def _rotary_a(t, cos, slo, shi):
    return (t * cos + pltpu.roll(t, LANES - A_ROT_DIM // 2, 1) * slo + pltpu.roll(t, A_ROT_DIM // 2, 1) * shi)


def _proj_a_kernel(h_ref, wqkv_ref, *rest, has_gate):
    if has_gate:
        wg_ref, cos_ref, slo_ref, shi_ref, o_ref = rest
    else:
        wg_ref, (cos_ref, slo_ref, shi_ref, o_ref) = None, rest
    h = h_ref[...]
    cos, slo, shi = cos_ref[...], slo_ref[...], shi_ref[...]
    scale = A_HEAD_DIM ** -0.5
    q_tabs = (cos * scale, slo * scale, shi * scale)
    for c in range(o_ref.shape[1] // MXU_COLS):
        tile, off = divmod(c * MXU_COLS, A_WIDTH)
        w_chunk = (wg_ref[:, off:off + MXU_COLS] if tile == 3
                   else wqkv_ref[:, c * MXU_COLS:(c + 1) * MXU_COLS])
        acc = jnp.dot(h, w_chunk, preferred_element_type=F32)
        for k in range(MXU_COLS // A_HEAD_DIM):
            t = acc[:, k * A_HEAD_DIM:(k + 1) * A_HEAD_DIM]
            if tile == 0:
                t = _rotary_a(t, *q_tabs)
            elif tile == 1:
                t = _rotary_a(t, cos, slo, shi)
            elif tile == 3:
                t = _silu(t)
            lo = c * MXU_COLS + k * A_HEAD_DIM
            o_ref[:, lo:lo + A_HEAD_DIM] = t.astype(BF16)


def _proj_a(h, w, tabs, g, tm=1024):
    m, d = h.shape
    has_gate = g == 0
    n_out = (4 if has_gate else 3) * A_WIDTH
    gate_tile = w.shape[1] // A_WIDTH - 1
    tab = pl.BlockSpec((tm, LANES), lambda i: (i, 0))
    w_specs = [_resident((d, 3 * A_WIDTH), lambda i: (0, g))]
    w_args = [w]
    if has_gate:
        w_specs.append(_resident((d, A_WIDTH), lambda i: (0, gate_tile)))
        w_args.append(w)
    return pl.pallas_call(
        functools.partial(_proj_a_kernel, has_gate=has_gate),
        out_shape=jax.ShapeDtypeStruct((m, n_out), BF16),
        grid=(m // tm,),
        in_specs=[pl.BlockSpec((tm, d), lambda i: (i, 0))] + w_specs + [tab, tab, tab],
        out_specs=pl.BlockSpec((tm, n_out), lambda i: (i, 0)),
        compiler_params=_params("parallel"),
        name=f"proj_a{g}",
    )(h, *w_args, *tabs)


def _proj_c_qk_kernel(h_ref, w_ref, cos_ref, sin_ref, o_ref):
    cos, sin = cos_ref[...], sin_ref[...]
    scale = C_QK_DIM ** -0.5
    k_tabs = (cos * scale, sin * scale)
    h = h_ref[...]
    half = C_QK_DIM // 2
    for c in range(o_ref.shape[1] // C_QK_DIM):
        acc = jnp.dot(h, w_ref[:, c * C_QK_DIM:(c + 1) * C_QK_DIM], preferred_element_type=F32)
        t1, t2 = acc[:, :half], acc[:, half:]
        cs, sn = (cos, sin) if c < C_HEADS else k_tabs
        o_ref[:, c * C_QK_DIM:c * C_QK_DIM + half] = (t1 * cs - t2 * sn).astype(BF16)
        o_ref[:, c * C_QK_DIM + half:(c + 1) * C_QK_DIM] = (t2 * cs + t1 * sn).astype(BF16)


def _proj_glu_kernel(h_ref, wa_ref, wb_ref, u_ref):
    h = h_ref[...]
    for c in range(u_ref.shape[1] // MXU_COLS):
        sl = slice(c * MXU_COLS, (c + 1) * MXU_COLS)
        a = jnp.dot(h, wa_ref[:, sl], preferred_element_type=F32)
        b = jnp.dot(h, wb_ref[:, sl], preferred_element_type=F32)
        u_ref[:, sl] = (a * jax.nn.sigmoid(b)).astype(BF16)


def _proj_act_kernel(h_ref, w_ref, o_ref, *, act):
    h = h_ref[...]
    for c in range(o_ref.shape[1] // MXU_COLS):
        sl = slice(c * MXU_COLS, (c + 1) * MXU_COLS)
        acc = jnp.dot(h, w_ref[:, sl], preferred_element_type=F32)
        o_ref[:, sl] = (_silu(acc) if act else acc).astype(BF16)


def _proj_cols(body, h, ws, first_tiles, tn, extra=(), name="proj", tm=1024):
    m, d = h.shape
    tab = pl.BlockSpec((tm, LANES), lambda i: (i, 0))
    w_specs = [_resident((d, tn), lambda i, f=f: (0, f)) for f in first_tiles]
    return pl.pallas_call(
        body,
        out_shape=jax.ShapeDtypeStruct((m, tn), BF16),
        grid=(m // tm,),
        in_specs=[pl.BlockSpec((tm, d), lambda i: (i, 0))] + w_specs + [tab] * len(extra),
        out_specs=pl.BlockSpec((tm, tn), lambda i: (i, 0)),
        compiler_params=_params("parallel"),
        name=name,
    )(h, *ws, *extra)


def _proj_c(h, w, tabs):
    tn = 2 * C_HEADS * C_QK_DIM
    qk = _proj_cols(_proj_c_qk_kernel, h, [w], [0], tn, tabs, "proj_c_qk")
    v = _proj_cols(functools.partial(_proj_act_kernel, act=False), h, [w], [1], tn, name="proj_c_v")
    gate = _proj_cols(functools.partial(_proj_act_kernel, act=True), h, [w], [2], tn, name="proj_c_gate")
    return qk, v, gate


def _proj_b_glu(h, w):
    return _proj_cols(_proj_glu_kernel, h, [w, w], [0, 1], w.shape[1] // 3, name="proj_b_glu")


def _attn_kernel(q_ref, k_ref, kp_ref, v_ref, vp_ref, o_ref, lse_ref, *, nb):
    first = pl.program_id(2) == 0
    row = lax.broadcasted_iota(jnp.int32, (A_BLOCK, 2 * A_BLOCK), 0)
    col = lax.broadcasted_iota(jnp.int32, (A_BLOCK, 2 * A_BLOCK), 1)
    in_prev = (col < A_BLOCK) & (col >= row)
    in_cur = (col >= A_BLOCK) & (col - A_BLOCK <= row)
    bias_inner = jnp.where(in_prev | in_cur, 0.0, -jnp.inf).astype(F32)
    bias_first = jnp.where((in_prev & jnp.logical_not(first)) | in_cur, 0.0, -jnp.inf).astype(F32)
    nt = (((1,), (1,)), ((), ()))
    heads = [slice(hd * A_HEAD_DIM, (hd + 1) * A_HEAD_DIM) for hd in range(A_HEADS)]
    items = [(g, hd) for g in range(nb) for hd in range(A_HEADS)]

    def keys(ref, prev_ref, g, sl):
        prev = prev_ref[:, sl] if g == 0 else ref[g - 1, :, sl]
        return jnp.concatenate([prev, ref[g, :, sl]], axis=0)

    scores = [lax.dot_general(q_ref[g, :, heads[hd]], keys(k_ref, kp_ref, g, heads[hd]), nt,
                              preferred_element_type=F32) + (bias_first if g == 0 else bias_inner)
              for g, hd in items]
    maxes = [jnp.max(s, axis=1, keepdims=True) for s in scores]
    probs = [jnp.exp(s - mx) for s, mx in zip(scores, maxes)]
    sums = [jnp.sum(p, axis=1, keepdims=True) for p in probs]
    outs = [jnp.dot(p.astype(BF16), keys(v_ref, vp_ref, g, heads[hd]), preferred_element_type=F32)
            for (g, hd), p in zip(items, probs)]
    lse_ref[...] = jnp.zeros(lse_ref.shape, F32)
    for n, (g, hd) in enumerate(items):
        o_ref[g, :, heads[hd]] = (outs[n] / sums[n]).astype(BF16)
        lse_ref[g, :, hd:hd + 1] = maxes[n] + jnp.log(sums[n])


def _attention_group(proj, batch, seq, dil, nb=8):
    m = batch * seq
    nblk = seq // (A_BLOCK * dil)
    nb = math.gcd(nb, nblk)
    steps = nblk // nb
    view = proj.reshape(batch * nblk, dil, A_BLOCK, proj.shape[1])

    def cur(tile):
        return pl.BlockSpec((nb, None, A_BLOCK, A_WIDTH), lambda b, r, t: (b * steps + t, r, 0, tile))

    def prev(tile):
        return pl.BlockSpec((None, None, A_BLOCK, A_WIDTH),
                            lambda b, r, t: (b * nblk + jnp.maximum(t * nb - 1, 0), r, 0, tile))

    o, lse = pl.pallas_call(
        functools.partial(_attn_kernel, nb=nb),
        out_shape=(jax.ShapeDtypeStruct((batch * nblk, dil, A_BLOCK, A_WIDTH), BF16),
                   jax.ShapeDtypeStruct((batch * nblk, dil, A_BLOCK, LANES), F32)),
        grid=(batch, dil, steps),
        in_specs=[cur(0), cur(1), prev(1), cur(2), prev(2)],
        out_specs=(pl.BlockSpec((nb, None, A_BLOCK, A_WIDTH), lambda b, r, t: (b * steps + t, r, 0, 0)),
                   pl.BlockSpec((nb, None, A_BLOCK, LANES), lambda b, r, t: (b * steps + t, r, 0, 0))),
        compiler_params=_params("parallel", "parallel", "arbitrary"),
        name=f"attn_d{dil}",
    )(view, view, view, view, view)
    return o.reshape(m, A_WIDTH), lse.reshape(m, LANES)


def _finish(y, x_ref, nw_ref, out_refs, slabs, last, dils):
    x_new = x_ref[...] + y
    h = _rms(x_new, nw_ref[...])
    if last:
        out_refs[0][...] = h
    else:
        out_refs[0][...] = x_new
        _write_perms(h, out_refs[1:], slabs, dils)


def _tail_outputs(m, d, tm, last, dils, step):
    x_shape = jax.ShapeDtypeStruct((m, d), F32)
    x_spec = pl.BlockSpec((tm, d), lambda *g: (step(*g), 0))
    if last:
        return [x_shape], [x_spec]
    shapes, specs = _perm_outputs(m, d, BF16, tm, dils, step)
    return [x_shape] + shapes, [x_spec] + specs


def _split_tail(rest, n_scratch, last, dils):
    n_out = 1 if last else 2 + len(dils)
    return rest[:n_out], rest[n_out:len(rest) - n_scratch], rest[len(rest) - n_scratch:]


def _gather_residue_major(ref, slab_ref, tmp_ref=None, inner=1):
    n_sub, dil, rows, c = ref.shape
    span = dil * rows
    outer = dil // inner
    for s in range(n_sub):
        for r in range(dil):
            v = ref[s, r].astype(F32)
            for k in range(c // LANES):
                piece = v[:, k * LANES:(k + 1) * LANES]
                if inner == 1:
                    slab_ref[k, pl.ds(s * span + r, rows, stride=dil), :] = piece
                else:
                    base = s * span + (r % inner) * (span // inner) + r // inner
                    tmp_ref[k, pl.ds(base, rows, stride=outer), :] = piece
    if inner > 1:
        for k in range(c // LANES):
            for s in range(n_sub):
                for r in range(inner):
                    lo = s * span + r * (span // inner)
                    slab_ref[k, pl.ds(s * span + r, span // inner, stride=inner), :] = tmp_ref[k, lo:lo + span // inner, :]


def _out_a_kernel(o1_ref, o2_ref, o3_ref, l1_ref, l2_ref, l3_ref, g_ref, x_ref, w_ref, nw_ref, *rest, last, dils):
    out_refs, slabs, (y_ref, y_new, acc_ref, os2, os3, o_tmp, ls2, ls3) = _split_tail(rest, 8, last, dils)

    @pl.when(pl.program_id(0) == 0)
    def _():
        y_ref[...] = jnp.zeros(y_ref.shape, BF16)

    inner = A_DILATIONS[1]
    _gather_residue_major(o2_ref, os2)
    _gather_residue_major(o3_ref, os3, o_tmp, inner)
    _gather_residue_major(l2_ref, ls2)
    _gather_residue_major(l3_ref, ls3, o_tmp, inner)
    l1, l2, l3 = l1_ref[:, :A_HEADS], ls2[0, :, :A_HEADS], ls3[0, :, :A_HEADS]
    mx = jnp.maximum(jnp.maximum(l1, l2), l3)
    e1, e2, e3 = jnp.exp(l1 - mx), jnp.exp(l2 - mx), jnp.exp(l3 - mx)
    den = e1 + e2 + e3
    a1, a2, a3 = e1 / den, e2 / den, e3 / den
    y_prev = y_ref[...]
    zero = jnp.zeros((1, LANES), F32)
    n_chunks = acc_ref.shape[1] // MXU_COLS
    for hd in range(A_HEADS):
        for cc in range(hd * n_chunks // A_HEADS, (hd + 1) * n_chunks // A_HEADS):
            cols = slice(cc * MXU_COLS, (cc + 1) * MXU_COLS)
            proj = jnp.dot(y_prev, w_ref[:, cols], preferred_element_type=F32)
            acc_ref[:, cols] = proj
        sl = slice(hd * A_HEAD_DIM, (hd + 1) * A_HEAD_DIM)
        c = slice(hd, hd + 1)
        o = a1[:, c] * o1_ref[:, sl].astype(F32) + a2[:, c] * os2[hd] + a3[:, c] * os3[hd] + zero
        y_new[:, sl] = (o * g_ref[:, sl].astype(F32)).astype(BF16)
        zero = _order_after(proj)
    _finish(acc_ref[...], x_ref, nw_ref, out_refs, slabs, last, dils)
    y_ref[...] = y_new[...]


def _residue_major_input(a, tm, dil, tile):
    m, c = a.shape
    shape, block, n_part = _perm_geometry(m, tm, dil)
    spec = pl.BlockSpec(block + (c,), lambda s: (tile(s) // n_part, 0, 0, tile(s) % n_part, 0, 0))
    return a.reshape(shape + (c,)), spec


def _out_a(os_, lses, proj0, x, w, nw, last, dils, tm=512):
    m, d = x.shape
    n_tiles = m // tm
    cur = lambda s: jnp.minimum(s, n_tiles - 1)
    lag = lambda s: jnp.maximum(s - 1, 0)
    row = lambda s: (cur(s), 0)
    o2, o2_spec = _residue_major_input(os_[1], tm, A_DILATIONS[1], cur)
    o3, o3_spec = _residue_major_input(os_[2], tm, A_DILATIONS[2], cur)
    l2, l2_spec = _residue_major_input(lses[1], tm, A_DILATIONS[1], cur)
    l3, l3_spec = _residue_major_input(lses[2], tm, A_DILATIONS[2], cur)
    shapes, specs = _tail_outputs(m, d, tm, last, dils, lag)
    o_slab = pltpu.VMEM((A_HEADS, tm, LANES), F32)
    l_slab = pltpu.VMEM((1, tm, LANES), F32)
    y_buf = pltpu.VMEM((tm, A_WIDTH), BF16)
    return pl.pallas_call(
        functools.partial(_out_a_kernel, last=last, dils=dils),
        out_shape=shapes,
        grid=(n_tiles + 1,),
        in_specs=[pl.BlockSpec((tm, A_WIDTH), row), o2_spec, o3_spec,
                  pl.BlockSpec((tm, LANES), row), l2_spec, l3_spec,
                  pl.BlockSpec((tm, A_WIDTH), lambda s: (cur(s), 3)),
                  pl.BlockSpec((tm, d), lambda s: (lag(s), 0)),
                  _resident((A_WIDTH, d), lambda s: (0, 0)), _resident((1, d), lambda s: (0, 0))],
        out_specs=specs,
        scratch_shapes=([] if last else _slab_scratch(d, tm, dils))
        + [y_buf, y_buf, pltpu.VMEM((tm, d), F32), o_slab, o_slab, o_slab, l_slab, l_slab],
        compiler_params=_params("arbitrary"),
        name="out_a",
    )(os_[0], o2, o3, lses[0], l2, l3, proj0, x, w, nw.reshape(1, d))


def _order_after(*values):
    bits = None
    for v in values:
        b = pltpu.bitcast(v[v.shape[0] - 8:, :LANES], jnp.uint32)
        bits = b if bits is None else bits | b
    sixteen = jnp.uint32(16)
    cleared = lax.shift_right_logical(lax.shift_right_logical(bits, sixteen), sixteen)
    return pltpu.bitcast(cleared, F32)[0:1, :]


def _out_b_kernel(u_ref, up_ref, h_ref, x_ref, wg_ref, cw_ref, cb_ref, lw_ref, lb_ref, w_ref, nw_ref, *rest,
                  last, tm, rc, tiles_per_seq, n_tiles):
    out_refs, (ubuf, cbuf, gbuf, y_ref, acc_ref) = rest[:-5], rest[-5:]
    s = pl.program_id(0)

    @pl.when(s == 0)
    def _():
        y_ref[...] = jnp.zeros(y_ref.shape, BF16)

    tile = jnp.minimum(s, n_tiles - 1)
    n_slab = ubuf.shape[0]
    halo = jnp.where(tile % tiles_per_seq > 0, up_ref[...].astype(F32), 0.0)
    for c in range(n_slab):
        sl = slice(c * LANES, (c + 1) * LANES)
        ubuf[c, 0:B_HALO, :] = halo[:, sl]
        ubuf[c, B_HALO:, :] = u_ref[:, sl].astype(F32)
    first = B_HALO - (B_CONV_WIDTH - 1)
    h = h_ref[...]
    y_prev = y_ref[...]
    per = MXU_COLS // LANES
    zero = jnp.zeros((1, LANES), F32)
    for cc in range(n_slab // per):
        cols = slice(cc * MXU_COLS, (cc + 1) * MXU_COLS)
        gate = jnp.dot(h, wg_ref[:, cols], preferred_element_type=F32)
        proj = jnp.dot(y_prev, w_ref[:, cols], preferred_element_type=F32)
        gbuf[:, cols] = _silu(gate).astype(BF16)
        acc_ref[:, cols] = proj
        for c in range(cc * per, (cc + 1) * per):
            sl = slice(c * LANES, (c + 1) * LANES)
            taps = [cw_ref[j:j + 1, sl] for j in range(B_CONV_WIDTH)]
            for r0 in range(0, tm, rc):
                acc = jnp.broadcast_to(cb_ref[:, sl] + zero, (rc, LANES))
                for j in range(B_CONV_WIDTH):
                    acc = acc + ubuf[c, r0 + first + j:r0 + first + j + rc, :] * taps[j]
                cbuf[r0:r0 + rc, sl] = acc
        zero = _order_after(gate, proj)
    x_new = x_ref[...] + acc_ref[...]
    h_new = _rms(x_new, nw_ref[...])
    if last:
        out_refs[0][...] = h_new
    else:
        out_refs[0][...] = x_new
        out_refs[1][...] = h_new.astype(BF16)
    cv = cbuf[...]
    mu = jnp.mean(cv, axis=-1, keepdims=True)
    cen = cv - mu
    var = jnp.mean(cen * cen, axis=-1, keepdims=True)
    cn = cen * lax.rsqrt(var + EPS) * lw_ref[...] + lb_ref[...]
    y_ref[...] = (_silu(cn) * gbuf[...].astype(F32)).astype(BF16)


def _out_b(u, h, x, w_in, conv_w, conv_b, ln_w, ln_b, w, nw, seq, last, tm=256, rc=64):
    m, d = x.shape
    width = u.shape[1]
    tiles_per_seq = seq // tm
    n_tiles = m // tm
    hb = tm // B_HALO
    cur = lambda s: (jnp.minimum(s, n_tiles - 1), 0)
    lag = lambda s: (jnp.maximum(s - 1, 0), 0)
    vec = lambda a: a.reshape(1, -1)
    const = lambda s: (0, 0)
    shapes, specs = _tail_outputs(m, d, tm, last, (), lambda s: jnp.maximum(s - 1, 0))
    return pl.pallas_call(
        functools.partial(_out_b_kernel, last=last, tm=tm, rc=rc, tiles_per_seq=tiles_per_seq, n_tiles=n_tiles),
        out_shape=shapes,
        grid=(n_tiles + 1,),
        in_specs=[pl.BlockSpec((tm, width), cur),
                  pl.BlockSpec((B_HALO, width), lambda s: (jnp.maximum(jnp.minimum(s, n_tiles - 1) * hb - 1, 0), 0)),
                  pl.BlockSpec((tm, d), cur),
                  pl.BlockSpec((tm, d), lag),
                  _resident((d, width), lambda s: (0, 2)),
                  _resident((B_CONV_WIDTH, width), const), _resident((1, width), const),
                  _resident((1, width), const), _resident((1, width), const),
                  _resident((width, d), const), _resident((1, d), const)],
        out_specs=specs,
        scratch_shapes=[pltpu.VMEM((width // LANES, tm + B_HALO, LANES), F32),
                        pltpu.VMEM((tm, width), F32), pltpu.VMEM((tm, width), BF16),
                        pltpu.VMEM((tm, width), BF16), pltpu.VMEM((tm, d), F32)],
        compiler_params=_params("arbitrary"),
        name="out_b",
    )(u, u, h, x, w_in, conv_w, vec(conv_b), vec(ln_w), vec(ln_b), w, vec(nw))


def _retention_kernel(q_ref, k_ref, v_ref, g_ref, dm_ref, qd_ref, kd_ref, cd_ref, y_ref, state, *, tc, ch):
    @pl.when(pl.program_id(2) == 0)
    def _():
        state[...] = jnp.zeros(state.shape, F32)

    nt = (((1,), (1,)), ((), ()))
    tn = (((0,), (0,)), ((), ()))
    dm, qd, kd, cd = dm_ref[...], qd_ref[...], kd_ref[...], cd_ref[...]
    n = tc // ch
    rows = [slice(c * ch, (c + 1) * ch) for c in range(n)]
    qs = [q_ref[rs, :] for rs in rows]
    ks = [k_ref[rs, :] for rs in rows]
    vs = [v_ref[rs, :] for rs in rows]
    inner = [(lax.dot_general(q, k, nt, preferred_element_type=F32) * dm).astype(BF16) for q, k in zip(qs, ks)]
    kdec = [(k.astype(F32) * kd).astype(BF16) for k in ks]
    intra = [jnp.dot(a, v, preferred_element_type=F32) for a, v in zip(inner, vs)]
    update = [lax.dot_general(a, v, tn, preferred_element_type=F32) for a, v in zip(kdec, vs)]
    st = state[...]
    for c in range(n):
        cross = jnp.dot(qs[c], st.astype(BF16), preferred_element_type=F32)
        o = intra[c] + cross * qd
        st = st * cd + update[c]
        if c == n - 1:
            state[...] = st
        mu = jnp.mean(o, axis=-1, keepdims=True)
        cen = o - mu
        var = jnp.mean(cen * cen, axis=-1, keepdims=True)
        on = cen * lax.rsqrt(var + EPS)
        y_ref[rows[c], :] = (on * g_ref[rows[c], :].astype(F32)).astype(BF16)


def _retention(qk, v, gate, batch, seq, tc=2048, ch=256):
    m = batch * seq
    nt = seq // tc
    vw = C_HEADS * C_V_DIM

    gammas = 1.0 - jnp.exp(jnp.linspace(math.log(1.0 / 32), math.log(1.0 / 512), C_HEADS, dtype=F32))
    log_g = jnp.log(gammas)
    idx = jnp.arange(ch, dtype=F32)
    diff = idx[:, None] - idx[None, :]
    decay_mask = jnp.where(diff >= 0, jnp.exp(jnp.maximum(diff, 0.0)[None] * log_g[:, None, None]), 0.0)
    q_decay = jnp.exp((idx[None] + 1.0) * log_g[:, None])
    k_decay = jnp.exp((ch - 1.0 - idx[None]) * log_g[:, None])
    chunk_decay = jnp.exp(ch * log_g)
    qd = jnp.broadcast_to(q_decay[:, :, None], (C_HEADS, ch, C_V_DIM))
    kd = jnp.broadcast_to(k_decay[:, :, None], (C_HEADS, ch, C_QK_DIM))
    cd = jnp.broadcast_to(chunk_decay[:, None, None], (C_HEADS, 1, C_V_DIM))

    head = lambda b, h, i: (h, 0, 0)
    rows = lambda off: (lambda b, h, i: (b * nt + i, off + h))
    return pl.pallas_call(
        functools.partial(_retention_kernel, tc=tc, ch=ch),
        out_shape=jax.ShapeDtypeStruct((m, vw), BF16),
        grid=(batch, C_HEADS, nt),
        in_specs=[pl.BlockSpec((tc, C_QK_DIM), rows(0)),
                  pl.BlockSpec((tc, C_QK_DIM), rows(C_HEADS)),
                  pl.BlockSpec((tc, C_V_DIM), rows(0)),
                  pl.BlockSpec((tc, C_V_DIM), rows(0)),
                  pl.BlockSpec((None, ch, ch), head),
                  pl.BlockSpec((None, ch, C_V_DIM), head),
                  pl.BlockSpec((None, ch, C_QK_DIM), head),
                  pl.BlockSpec((None, 1, C_V_DIM), head)],
        out_specs=pl.BlockSpec((tc, C_V_DIM), rows(0)),
        scratch_shapes=[pltpu.VMEM((C_QK_DIM, C_V_DIM), F32)],
        compiler_params=_params("parallel", "parallel", "arbitrary"),
        name="retention",
    )(qk, qk, v, gate, decay_mask, qd, kd, cd)


def _out_c_kernel(y_ref, x_ref, w_ref, nw_ref, *rest, last, dils):
    out_refs, slabs, _ = _split_tail(rest, 0, last, dils)
    y = jnp.dot(y_ref[...], w_ref[...], preferred_element_type=F32)
    _finish(y, x_ref, nw_ref, out_refs, slabs, last, dils)


def _out_c(y, x, w, nw, last, dils, tm=256):
    m, d = x.shape
    kdim = y.shape[1]
    row = lambda i: (i, 0)
    shapes, specs = _tail_outputs(m, d, tm, last, dils, lambda i: i)
    return pl.pallas_call(
        functools.partial(_out_c_kernel, last=last, dils=dils),
        out_shape=shapes,
        grid=(m // tm,),
        in_specs=[pl.BlockSpec((tm, kdim), row), pl.BlockSpec((tm, d), row),
                  _resident((kdim, d), lambda i: (0, 0)), _resident((1, d), lambda i: (0, 0))],
        out_specs=specs,
        scratch_shapes=[] if last else _slab_scratch(d, tm, dils),
        compiler_params=_params("parallel"),
        name="out_c",
    )(y, x, w, nw.reshape(1, d))


def kernel(x, positions, norm_w, final_norm_w, a_w_in, a_w_out, b_w_in, b_conv_w, b_conv_b, b_ln_w, b_ln_b,
           b_w_out, c_w_in, c_w_out):
    batch, seq, d = x.shape
    depth = norm_w.shape[0]
    m = batch * seq
    xs = x.reshape(m, d)
    perm_dils = tuple(dl for dl in A_DILATIONS if dl > 1)

    def next_dils(i):
        return perm_dils if (i + 1 < depth and (i + 1) % 3 == 0) else ()

    fa, fc = _rope_freqs()
    pos = positions.reshape(m, 1)
    tabs_a = _rope_a_tables(pos, fa, perm_dils)
    tabs_c = _rope_c_tables(pos, fc)

    hs = _rmsnorm(xs, norm_w[0], perm_dils)
    for i in range(depth):
        kind, j = i % 3, i // 3
        last = i == depth - 1
        nw = final_norm_w if last else norm_w[i + 1]
        dils = next_dils(i)
        if kind == 0:
            w_in = a_w_in[j].astype(BF16)
            os_, lses, proj0 = [], [], None
            for g, dl in enumerate(A_DILATIONS):
                h_g = hs[g].reshape(m, d)
                proj = _proj_a(h_g, w_in, tabs_a[g], g)
                proj0 = proj if g == 0 else proj0
                o, lse = _attention_group(proj, batch, seq, dl)
                os_.append(o)
                lses.append(lse)
            outs = _out_a(os_, lses, proj0, xs, a_w_out[j].astype(BF16), nw, last, dils)
        elif kind == 1:
            assert not dils
            w_in = b_w_in[j].astype(BF16)
            u = _proj_b_glu(hs[0], w_in)
            outs = _out_b(u, hs[0], xs, w_in, b_conv_w[j], b_conv_b[j], b_ln_w[j], b_ln_b[j],
                          b_w_out[j].astype(BF16), nw, seq, last)
        else:
            qk, v, gate = _proj_c(hs[0], c_w_in[j].astype(BF16), tabs_c)
            y = _retention(qk, v, gate, batch, seq)
            outs = _out_c(y, xs, c_w_out[j].astype(BF16), nw, last, dils)
        if last:
            return outs[0].reshape(batch, seq, d)
        xs, hs = outs[0], outs[1:]
```

```python
import functools
import math
from typing import NamedTuple

import jax
import jax.numpy as jnp
from jax import lax
from jax.experimental import pallas as pl
from jax.experimental.pallas import tpu as pltpu

F32 = jnp.float32
BF16 = jnp.bfloat16

EPS = 1e-6
LANES = 128
SUBLANES = 8
MXU_COLS = 256
V7X_VMEM_LIMIT = 56 * 1024 * 1024


class _Tiles(NamedTuple):
    norm: int = 512
    rope: int = 1024
    proj: int = 1024
    attn_blocks: int = 8
    out_a: int = 512
    out_b: int = 256
    conv_rows: int = 64
    retention: int = 2048
    retention_chunk: int = 256
    out_c: int = 256


TILES = _Tiles()

A_DILATIONS = (1, 4, 16)
A_HEADS = 8
A_HEAD_DIM = 128
A_WIDTH = A_HEADS * A_HEAD_DIM
A_ROT_DIM = A_HEAD_DIM // 4
A_ROPE_THETA = 500000.0
A_BLOCK = 128
B_CONV_WIDTH = 31
B_HALO = 32
C_QK_DIM = 256
C_HEADS = 8
C_V_DIM = 512
C_ROPE_THETA = 10000.0


def _params(*sem):
    return pltpu.CompilerParams(dimension_semantics=sem, vmem_limit_bytes=V7X_VMEM_LIMIT)


def _resident(shape, index_map):
    return pl.BlockSpec(shape, index_map, pipeline_mode=pl.Buffered(1))


def _rms(x, w):
    y = x * lax.rsqrt(jnp.mean(x * x, axis=-1, keepdims=True) + EPS)
    return y * w


def _silu(x):
    return x * jax.nn.sigmoid(x)


def _perm_geometry(m, tm, dil):
    tile = A_BLOCK * dil
    big = max(tm, tile)
    n_sub, n_part, rows = big // tile, big // tm, min(tm, tile) // dil
    return (m // big, n_sub, dil, n_part, rows), (None, n_sub, dil, None, rows), n_part


def _perm_outputs(m, c, dtype, tm, dils, step):
    shapes = [jax.ShapeDtypeStruct((m, c), dtype)]
    specs = [pl.BlockSpec((tm, c), lambda *g: (step(*g), 0))]
    for dil in dils:
        shape, block, n_part = _perm_geometry(m, tm, dil)
        shapes.append(jax.ShapeDtypeStruct(shape + (c,), dtype))
        specs.append(pl.BlockSpec(block + (c,),
                                  lambda *g, n_part=n_part: (step(*g) // n_part, 0, 0, step(*g) % n_part, 0, 0)))
    return shapes, specs


def _write_perms(v, refs, slabs, dils):
    dtype = refs[0].dtype
    refs[0][...] = v.astype(dtype)
    if not dils:
        return
    tm = v.shape[0]
    n_slab = slabs[0].shape[0]
    for c in range(n_slab):
        slabs[0][c] = v[:, c * LANES:(c + 1) * LANES]
    span_prev, dil_prev = tm, 1
    for k, (ref, dil) in enumerate(zip(refs[1:], dils)):
        src, dst = slabs[k % 2], slabs[(k + 1) % 2]
        n_sub, _, rows, _ = ref.shape
        span = tm // n_sub
        step = dil // dil_prev
        pieces = span // span_prev if dil_prev > 1 else 1
        span_src = span // pieces
        rows_src = span_src // dil_prev
        for c in range(n_slab):
            sl = slice(c * LANES, (c + 1) * LANES)
            for s in range(n_sub):
                for r in range(dil):
                    r_prev, r_new = r % dil_prev, r // dil_prev
                    for p in range(pieces):
                        base = (s * pieces + p) * span_src + r_prev * rows_src + r_new
                        n = rows_src // step
                        t = src[c, pl.ds(base, n, stride=step), :]
                        ref[s, r, p * n:(p + 1) * n, sl] = t.astype(dtype)
                        if k + 1 < len(dils):
                            dst[c, s * span + r * rows + p * n:s * span + r * rows + (p + 1) * n, :] = t
        span_prev, dil_prev = span, dil


def _slab_scratch(d, tm, dils):
    return [pltpu.VMEM((d // LANES, tm, LANES), F32)] * min(len(dils), 2)


def _rmsnorm_kernel(x_ref, w_ref, *rest, dils):
    n_out = 1 + len(dils)
    _write_perms(_rms(x_ref[...], w_ref[...]), rest[:n_out], rest[n_out:], dils)


def _rmsnorm(x, w, dils, tm=TILES.norm):
    m, d = x.shape
    shapes, specs = _perm_outputs(m, d, BF16, tm, dils, lambda i: i)
    return pl.pallas_call(
        functools.partial(_rmsnorm_kernel, dils=dils),
        out_shape=shapes,
        grid=(m // tm,),
        in_specs=[pl.BlockSpec((tm, d), lambda i: (i, 0)), _resident((1, d), lambda i: (0, 0))],
        out_specs=specs,
        scratch_shapes=_slab_scratch(d, tm, dils),
        compiler_params=_params("parallel"),
        name="rmsnorm_in",
    )(x, w.reshape(1, d))


def _rope_a_kernel(pos_ref, f_ref, *rest, dils):
    n_out = 3 * (1 + len(dils))
    outs, slabs = rest[:n_out], rest[n_out:]
    ang = pos_ref[...].astype(F32) * f_ref[...]
    sin = jnp.sin(ang)
    lane = lax.broadcasted_iota(jnp.int32, ang.shape, 1)
    half = A_ROT_DIM // 2
    tables = (jnp.cos(ang),
              jnp.where(lane < half, -sin, 0.0),
              jnp.where((lane >= half) & (lane < A_ROT_DIM), sin, 0.0))
    n = 1 + len(dils)
    for k, tab in enumerate(tables):
        _write_perms(tab, outs[k * n:(k + 1) * n], slabs, dils)


def _rope_c_kernel(pos_ref, f_ref, cos_ref, sin_ref):
    ang = pos_ref[...].astype(F32) * f_ref[...]
    cos_ref[...] = jnp.cos(ang)
    sin_ref[...] = jnp.sin(ang)


def _rope_a_tables(pos, freqs, dils, tm=TILES.rope):
    m = pos.shape[0]
    shapes, specs = _perm_outputs(m, LANES, F32, tm, dils, lambda i: i)
    outs = pl.pallas_call(
        functools.partial(_rope_a_kernel, dils=dils),
        out_shape=shapes * 3,
        grid=(m // tm,),
        in_specs=[pl.BlockSpec((tm, 1), lambda i: (i, 0)), _resident((1, LANES), lambda i: (0, 0))],
        out_specs=specs * 3,
        scratch_shapes=_slab_scratch(LANES, tm, dils),
        compiler_params=_params("parallel"),
        name="rope_a",
    )(pos, freqs)
    n = 1 + len(dils)
    return [tuple(outs[k * n + o].reshape(m, LANES) for k in range(3)) for o in range(n)]


def _rope_c_tables(pos, freqs, tm=TILES.rope):
    m = pos.shape[0]
    row = pl.BlockSpec((tm, LANES), lambda i: (i, 0))
    return pl.pallas_call(
        _rope_c_kernel,
        out_shape=(jax.ShapeDtypeStruct((m, LANES), F32),) * 2,
        grid=(m // tm,),
        in_specs=[pl.BlockSpec((tm, 1), lambda i: (i, 0)), _resident((1, LANES), lambda i: (0, 0))],
        out_specs=(row, row),
        compiler_params=_params("parallel"),
        name="rope_c",
    )(pos, freqs)


def _rope_freqs():
    half_a = A_ROT_DIM // 2
    inv_a = A_ROPE_THETA ** (-jnp.arange(0, A_ROT_DIM, 2, dtype=F32) / A_ROT_DIM)
    fa = jnp.concatenate([inv_a, inv_a, jnp.zeros((LANES - 2 * half_a,), F32)]).reshape(1, LANES)
    fc = (C_ROPE_THETA ** (-jnp.arange(0, C_QK_DIM, 2, dtype=F32) / C_QK_DIM)).reshape(1, LANES)
    return fa, fc


def _rotary_a(t, cos, slo, shi):
    return (t * cos + pltpu.roll(t, LANES - A_ROT_DIM // 2, 1) * slo + pltpu.roll(t, A_ROT_DIM // 2, 1) * shi)


def _proj_a_kernel(h_ref, wqkv_ref, *rest, has_gate):
    if has_gate:
        wg_ref, cos_ref, slo_ref, shi_ref, o_ref = rest
    else:
        wg_ref, (cos_ref, slo_ref, shi_ref, o_ref) = None, rest
    h = h_ref[...]
    cos, slo, shi = cos_ref[...], slo_ref[...], shi_ref[...]
    scale = A_HEAD_DIM ** -0.5
    q_tabs = (cos * scale, slo * scale, shi * scale)
    for c in range(o_ref.shape[1] // MXU_COLS):
        tile, off = divmod(c * MXU_COLS, A_WIDTH)
        w_chunk = (wg_ref[:, off:off + MXU_COLS] if tile == 3
                   else wqkv_ref[:, c * MXU_COLS:(c + 1) * MXU_COLS])
        acc = jnp.dot(h, w_chunk, preferred_element_type=F32)
        for k in range(MXU_COLS // A_HEAD_DIM):
            t = acc[:, k * A_HEAD_DIM:(k + 1) * A_HEAD_DIM]
            if tile == 0:
                t = _rotary_a(t, *q_tabs)
            elif tile == 1:
                t = _rotary_a(t, cos, slo, shi)
            elif tile == 3:
                t = _silu(t)
            lo = c * MXU_COLS + k * A_HEAD_DIM
            o_ref[:, lo:lo + A_HEAD_DIM] = t.astype(BF16)


def _proj_a(h, w, tabs, g, tm=TILES.proj):
    m, d = h.shape
    has_gate = g == 0
    n_out = (4 if has_gate else 3) * A_WIDTH
    gate_tile = w.shape[1] // A_WIDTH - 1
    tab = pl.BlockSpec((tm, LANES), lambda i: (i, 0))
    w_specs = [_resident((d, 3 * A_WIDTH), lambda i: (0, g))]
    w_args = [w]
    if has_gate:
        w_specs.append(_resident((d, A_WIDTH), lambda i: (0, gate_tile)))
        w_args.append(w)
    return pl.pallas_call(
        functools.partial(_proj_a_kernel, has_gate=has_gate),
        out_shape=jax.ShapeDtypeStruct((m, n_out), BF16),
        grid=(m // tm,),
        in_specs=[pl.BlockSpec((tm, d), lambda i: (i, 0))] + w_specs + [tab, tab, tab],
        out_specs=pl.BlockSpec((tm, n_out), lambda i: (i, 0)),
        compiler_params=_params("parallel"),
        name=f"proj_a{g}",
    )(h, *w_args, *tabs)


def _proj_c_qk_kernel(h_ref, w_ref, cos_ref, sin_ref, o_ref):
    cos, sin = cos_ref[...], sin_ref[...]
    scale = C_QK_DIM ** -0.5
    k_tabs = (cos * scale, sin * scale)
    h = h_ref[...]
    half = C_QK_DIM // 2
    for c in range(o_ref.shape[1] // C_QK_DIM):
        acc = jnp.dot(h, w_ref[:, c * C_QK_DIM:(c + 1) * C_QK_DIM], preferred_element_type=F32)
        t1, t2 = acc[:, :half], acc[:, half:]
        cs, sn = (cos, sin) if c < C_HEADS else k_tabs
        o_ref[:, c * C_QK_DIM:c * C_QK_DIM + half] = (t1 * cs - t2 * sn).astype(BF16)
        o_ref[:, c * C_QK_DIM + half:(c + 1) * C_QK_DIM] = (t2 * cs + t1 * sn).astype(BF16)


def _proj_glu_kernel(h_ref, wa_ref, wb_ref, u_ref):
    h = h_ref[...]
    for c in range(u_ref.shape[1] // MXU_COLS):
        sl = slice(c * MXU_COLS, (c + 1) * MXU_COLS)
        a = jnp.dot(h, wa_ref[:, sl], preferred_element_type=F32)
        b = jnp.dot(h, wb_ref[:, sl], preferred_element_type=F32)
        u_ref[:, sl] = (a * jax.nn.sigmoid(b)).astype(BF16)


def _proj_act_kernel(h_ref, w_ref, o_ref, *, act):
    h = h_ref[...]
    for c in range(o_ref.shape[1] // MXU_COLS):
        sl = slice(c * MXU_COLS, (c + 1) * MXU_COLS)
        acc = jnp.dot(h, w_ref[:, sl], preferred_element_type=F32)
        o_ref[:, sl] = (_silu(acc) if act else acc).astype(BF16)


def _proj_cols(body, h, ws, first_tiles, tn, extra=(), name="proj", tm=TILES.proj):
    m, d = h.shape
    tab = pl.BlockSpec((tm, LANES), lambda i: (i, 0))
    w_specs = [_resident((d, tn), lambda i, f=f: (0, f)) for f in first_tiles]
    return pl.pallas_call(
        body,
        out_shape=jax.ShapeDtypeStruct((m, tn), BF16),
        grid=(m // tm,),
        in_specs=[pl.BlockSpec((tm, d), lambda i: (i, 0))] + w_specs + [tab] * len(extra),
        out_specs=pl.BlockSpec((tm, tn), lambda i: (i, 0)),
        compiler_params=_params("parallel"),
        name=name,
    )(h, *ws, *extra)


def _proj_c(h, w, tabs):
    tn = 2 * C_HEADS * C_QK_DIM
    qk = _proj_cols(_proj_c_qk_kernel, h, [w], [0], tn, tabs, "proj_c_qk")
    v = _proj_cols(functools.partial(_proj_act_kernel, act=False), h, [w], [1], tn, name="proj_c_v")
    gate = _proj_cols(functools.partial(_proj_act_kernel, act=True), h, [w], [2], tn, name="proj_c_gate")
    return qk, v, gate


def _proj_b_glu(h, w):
    return _proj_cols(_proj_glu_kernel, h, [w, w], [0, 1], w.shape[1] // 3, name="proj_b_glu")


def _attn_kernel(q_ref, k_ref, kp_ref, v_ref, vp_ref, o_ref, lse_ref, *, nb):
    first = pl.program_id(2) == 0
    row = lax.broadcasted_iota(jnp.int32, (A_BLOCK, 2 * A_BLOCK), 0)
    col = lax.broadcasted_iota(jnp.int32, (A_BLOCK, 2 * A_BLOCK), 1)
    in_prev = (col < A_BLOCK) & (col >= row)
    in_cur = (col >= A_BLOCK) & (col - A_BLOCK <= row)
    bias_inner = jnp.where(in_prev | in_cur, 0.0, -jnp.inf).astype(F32)
    bias_first = jnp.where((in_prev & jnp.logical_not(first)) | in_cur, 0.0, -jnp.inf).astype(F32)
    nt = (((1,), (1,)), ((), ()))
    heads = [slice(hd * A_HEAD_DIM, (hd + 1) * A_HEAD_DIM) for hd in range(A_HEADS)]
    items = [(g, hd) for g in range(nb) for hd in range(A_HEADS)]

    def keys(ref, prev_ref, g, sl):
        prev = prev_ref[:, sl] if g == 0 else ref[g - 1, :, sl]
        return jnp.concatenate([prev, ref[g, :, sl]], axis=0)

    scores = [lax.dot_general(q_ref[g, :, heads[hd]], keys(k_ref, kp_ref, g, heads[hd]), nt,
                              preferred_element_type=F32) + (bias_first if g == 0 else bias_inner)
              for g, hd in items]
    maxes = [jnp.max(s, axis=1, keepdims=True) for s in scores]
    probs = [jnp.exp(s - mx) for s, mx in zip(scores, maxes)]
    sums = [jnp.sum(p, axis=1, keepdims=True) for p in probs]
    outs = [jnp.dot(p.astype(BF16), keys(v_ref, vp_ref, g, heads[hd]), preferred_element_type=F32)
            for (g, hd), p in zip(items, probs)]
    lse_ref[...] = jnp.zeros(lse_ref.shape, F32)
    for n, (g, hd) in enumerate(items):
        o_ref[g, :, heads[hd]] = (outs[n] / sums[n]).astype(BF16)
        lse_ref[g, :, hd:hd + 1] = maxes[n] + jnp.log(sums[n])


def _attention_group(proj, batch, seq, dil, nb=TILES.attn_blocks):
    m = batch * seq
    nblk = seq // (A_BLOCK * dil)
    nb = math.gcd(nb, nblk)
    steps = nblk // nb
    view = proj.reshape(batch * nblk, dil, A_BLOCK, proj.shape[1])

    def cur(tile):
        return pl.BlockSpec((nb, None, A_BLOCK, A_WIDTH), lambda b, r, t: (b * steps + t, r, 0, tile))

    def prev(tile):
        return pl.BlockSpec((None, None, A_BLOCK, A_WIDTH),
                            lambda b, r, t: (b * nblk + jnp.maximum(t * nb - 1, 0), r, 0, tile))

    o, lse = pl.pallas_call(
        functools.partial(_attn_kernel, nb=nb),
        out_shape=(jax.ShapeDtypeStruct((batch * nblk, dil, A_BLOCK, A_WIDTH), BF16),
                   jax.ShapeDtypeStruct((batch * nblk, dil, A_BLOCK, LANES), F32)),
        grid=(batch, dil, steps),
        in_specs=[cur(0), cur(1), prev(1), cur(2), prev(2)],
        out_specs=(pl.BlockSpec((nb, None, A_BLOCK, A_WIDTH), lambda b, r, t: (b * steps + t, r, 0, 0)),
                   pl.BlockSpec((nb, None, A_BLOCK, LANES), lambda b, r, t: (b * steps + t, r, 0, 0))),
        compiler_params=_params("parallel", "parallel", "arbitrary"),
        name=f"attn_d{dil}",
    )(view, view, view, view, view)
    return o.reshape(m, A_WIDTH), lse.reshape(m, LANES)


def _finish(y, x_ref, nw_ref, out_refs, slabs, last, dils):
    x_new = x_ref[...] + y
    h = _rms(x_new, nw_ref[...])
    if last:
        out_refs[0][...] = h
    else:
        out_refs[0][...] = x_new
        _write_perms(h, out_refs[1:], slabs, dils)


def _tail_outputs(m, d, tm, last, dils, step):
    x_shape = jax.ShapeDtypeStruct((m, d), F32)
    x_spec = pl.BlockSpec((tm, d), lambda *g: (step(*g), 0))
    if last:
        return [x_shape], [x_spec]
    shapes, specs = _perm_outputs(m, d, BF16, tm, dils, step)
    return [x_shape] + shapes, [x_spec] + specs


def _split_tail(rest, n_scratch, last, dils):
    n_out = 1 if last else 2 + len(dils)
    return rest[:n_out], rest[n_out:len(rest) - n_scratch], rest[len(rest) - n_scratch:]


def _gather_residue_major(ref, slab_ref, tmp_ref=None, inner=1):
    n_sub, dil, rows, c = ref.shape
    span = dil * rows
    outer = dil // inner
    for s in range(n_sub):
        for r in range(dil):
            v = ref[s, r].astype(F32)
            for k in range(c // LANES):
                piece = v[:, k * LANES:(k + 1) * LANES]
                if inner == 1:
                    slab_ref[k, pl.ds(s * span + r, rows, stride=dil), :] = piece
                else:
                    base = s * span + (r % inner) * (span // inner) + r // inner
                    tmp_ref[k, pl.ds(base, rows, stride=outer), :] = piece
    if inner > 1:
        for k in range(c // LANES):
            for s in range(n_sub):
                for r in range(inner):
                    lo = s * span + r * (span // inner)
                    slab_ref[k, pl.ds(s * span + r, span // inner, stride=inner), :] = tmp_ref[k, lo:lo + span // inner, :]


def _out_a_kernel(o1_ref, o2_ref, o3_ref, l1_ref, l2_ref, l3_ref, g_ref, x_ref, w_ref, nw_ref, *rest, last, dils):
    out_refs, slabs, (y_ref, y_new, acc_ref, os2, os3, o_tmp, ls2, ls3) = _split_tail(rest, 8, last, dils)

    @pl.when(pl.program_id(0) == 0)
    def _():
        y_ref[...] = jnp.zeros(y_ref.shape, BF16)

    inner = A_DILATIONS[1]
    _gather_residue_major(o2_ref, os2)
    _gather_residue_major(o3_ref, os3, o_tmp, inner)
    _gather_residue_major(l2_ref, ls2)
    _gather_residue_major(l3_ref, ls3, o_tmp, inner)
    l1, l2, l3 = l1_ref[:, :A_HEADS], ls2[0, :, :A_HEADS], ls3[0, :, :A_HEADS]
    mx = jnp.maximum(jnp.maximum(l1, l2), l3)
    e1, e2, e3 = jnp.exp(l1 - mx), jnp.exp(l2 - mx), jnp.exp(l3 - mx)
    den = e1 + e2 + e3
    a1, a2, a3 = e1 / den, e2 / den, e3 / den
    y_prev = y_ref[...]
    zero = jnp.zeros((1, LANES), F32)
    n_chunks = acc_ref.shape[1] // MXU_COLS
    for hd in range(A_HEADS):
        for cc in range(hd * n_chunks // A_HEADS, (hd + 1) * n_chunks // A_HEADS):
            cols = slice(cc * MXU_COLS, (cc + 1) * MXU_COLS)
            proj = jnp.dot(y_prev, w_ref[:, cols], preferred_element_type=F32)
            acc_ref[:, cols] = proj
        sl = slice(hd * A_HEAD_DIM, (hd + 1) * A_HEAD_DIM)
        c = slice(hd, hd + 1)
        o = a1[:, c] * o1_ref[:, sl].astype(F32) + a2[:, c] * os2[hd] + a3[:, c] * os3[hd] + zero
        y_new[:, sl] = (o * g_ref[:, sl].astype(F32)).astype(BF16)
        zero = _order_after(proj)
    _finish(acc_ref[...], x_ref, nw_ref, out_refs, slabs, last, dils)
    y_ref[...] = y_new[...]


def _residue_major_input(a, tm, dil, tile):
    m, c = a.shape
    shape, block, n_part = _perm_geometry(m, tm, dil)
    spec = pl.BlockSpec(block + (c,), lambda s: (tile(s) // n_part, 0, 0, tile(s) % n_part, 0, 0))
    return a.reshape(shape + (c,)), spec


def _out_a(os_, lses, proj0, x, w, nw, last, dils, tm=TILES.out_a):
    m, d = x.shape
    n_tiles = m // tm
    cur = lambda s: jnp.minimum(s, n_tiles - 1)
    lag = lambda s: jnp.maximum(s - 1, 0)
    row = lambda s: (cur(s), 0)
    o2, o2_spec = _residue_major_input(os_[1], tm, A_DILATIONS[1], cur)
    o3, o3_spec = _residue_major_input(os_[2], tm, A_DILATIONS[2], cur)
    l2, l2_spec = _residue_major_input(lses[1], tm, A_DILATIONS[1], cur)
    l3, l3_spec = _residue_major_input(lses[2], tm, A_DILATIONS[2], cur)
    shapes, specs = _tail_outputs(m, d, tm, last, dils, lag)
    o_slab = pltpu.VMEM((A_HEADS, tm, LANES), F32)
    l_slab = pltpu.VMEM((1, tm, LANES), F32)
    y_buf = pltpu.VMEM((tm, A_WIDTH), BF16)
    return pl.pallas_call(
        functools.partial(_out_a_kernel, last=last, dils=dils),
        out_shape=shapes,
        grid=(n_tiles + 1,),
        in_specs=[pl.BlockSpec((tm, A_WIDTH), row), o2_spec, o3_spec,
                  pl.BlockSpec((tm, LANES), row), l2_spec, l3_spec,
                  pl.BlockSpec((tm, A_WIDTH), lambda s: (cur(s), 3)),
                  pl.BlockSpec((tm, d), lambda s: (lag(s), 0)),
                  _resident((A_WIDTH, d), lambda s: (0, 0)), _resident((1, d), lambda s: (0, 0))],
        out_specs=specs,
        scratch_shapes=([] if last else _slab_scratch(d, tm, dils))
        + [y_buf, y_buf, pltpu.VMEM((tm, d), F32), o_slab, o_slab, o_slab, l_slab, l_slab],
        compiler_params=_params("arbitrary"),
        name="out_a",
    )(os_[0], o2, o3, lses[0], l2, l3, proj0, x, w, nw.reshape(1, d))


def _order_after(*values):
    bits = None
    for v in values:
        b = pltpu.bitcast(v[v.shape[0] - SUBLANES:, :LANES], jnp.uint32)
        bits = b if bits is None else bits | b
    half_word = jnp.uint32(16)
    cleared = lax.shift_right_logical(lax.shift_right_logical(bits, half_word), half_word)
    return pltpu.bitcast(cleared, F32)[0:1, :]


def _out_b_kernel(u_ref, up_ref, h_ref, x_ref, wg_ref, cw_ref, cb_ref, lw_ref, lb_ref, w_ref, nw_ref, *rest,
                  last, tm, rc, tiles_per_seq, n_tiles):
    out_refs, (ubuf, cbuf, gbuf, y_ref, acc_ref) = rest[:-5], rest[-5:]
    s = pl.program_id(0)

    @pl.when(s == 0)
    def _():
        y_ref[...] = jnp.zeros(y_ref.shape, BF16)

    tile = jnp.minimum(s, n_tiles - 1)
    n_slab = ubuf.shape[0]
    halo = jnp.where(tile % tiles_per_seq > 0, up_ref[...].astype(F32), 0.0)
    for c in range(n_slab):
        sl = slice(c * LANES, (c + 1) * LANES)
        ubuf[c, 0:B_HALO, :] = halo[:, sl]
        ubuf[c, B_HALO:, :] = u_ref[:, sl].astype(F32)
    first = B_HALO - (B_CONV_WIDTH - 1)
    h = h_ref[...]
    y_prev = y_ref[...]
    per = MXU_COLS // LANES
    zero = jnp.zeros((1, LANES), F32)
    for cc in range(n_slab // per):
        cols = slice(cc * MXU_COLS, (cc + 1) * MXU_COLS)
        gate = jnp.dot(h, wg_ref[:, cols], preferred_element_type=F32)
        proj = jnp.dot(y_prev, w_ref[:, cols], preferred_element_type=F32)
        gbuf[:, cols] = _silu(gate).astype(BF16)
        acc_ref[:, cols] = proj
        for c in range(cc * per, (cc + 1) * per):
            sl = slice(c * LANES, (c + 1) * LANES)
            taps = [cw_ref[j:j + 1, sl] for j in range(B_CONV_WIDTH)]
            for r0 in range(0, tm, rc):
                acc = jnp.broadcast_to(cb_ref[:, sl] + zero, (rc, LANES))
                for j in range(B_CONV_WIDTH):
                    acc = acc + ubuf[c, r0 + first + j:r0 + first + j + rc, :] * taps[j]
                cbuf[r0:r0 + rc, sl] = acc
        zero = _order_after(gate, proj)
    x_new = x_ref[...] + acc_ref[...]
    h_new = _rms(x_new, nw_ref[...])
    if last:
        out_refs[0][...] = h_new
    else:
        out_refs[0][...] = x_new
        out_refs[1][...] = h_new.astype(BF16)
    cv = cbuf[...]
    mu = jnp.mean(cv, axis=-1, keepdims=True)
    cen = cv - mu
    var = jnp.mean(cen * cen, axis=-1, keepdims=True)
    cn = cen * lax.rsqrt(var + EPS) * lw_ref[...] + lb_ref[...]
    y_ref[...] = (_silu(cn) * gbuf[...].astype(F32)).astype(BF16)


def _out_b(u, h, x, w_in, conv_w, conv_b, ln_w, ln_b, w, nw, seq, last, tm=TILES.out_b, rc=TILES.conv_rows):
    m, d = x.shape
    width = u.shape[1]
    tiles_per_seq = seq // tm
    n_tiles = m // tm
    hb = tm // B_HALO
    cur = lambda s: (jnp.minimum(s, n_tiles - 1), 0)
    lag = lambda s: (jnp.maximum(s - 1, 0), 0)
    vec = lambda a: a.reshape(1, -1)
    const = lambda s: (0, 0)
    shapes, specs = _tail_outputs(m, d, tm, last, (), lambda s: jnp.maximum(s - 1, 0))
    return pl.pallas_call(
        functools.partial(_out_b_kernel, last=last, tm=tm, rc=rc, tiles_per_seq=tiles_per_seq, n_tiles=n_tiles),
        out_shape=shapes,
        grid=(n_tiles + 1,),
        in_specs=[pl.BlockSpec((tm, width), cur),
                  pl.BlockSpec((B_HALO, width), lambda s: (jnp.maximum(jnp.minimum(s, n_tiles - 1) * hb - 1, 0), 0)),
                  pl.BlockSpec((tm, d), cur),
                  pl.BlockSpec((tm, d), lag),
                  _resident((d, width), lambda s: (0, 2)),
                  _resident((B_CONV_WIDTH, width), const), _resident((1, width), const),
                  _resident((1, width), const), _resident((1, width), const),
                  _resident((width, d), const), _resident((1, d), const)],
        out_specs=specs,
        scratch_shapes=[pltpu.VMEM((width // LANES, tm + B_HALO, LANES), F32),
                        pltpu.VMEM((tm, width), F32), pltpu.VMEM((tm, width), BF16),
                        pltpu.VMEM((tm, width), BF16), pltpu.VMEM((tm, d), F32)],
        compiler_params=_params("arbitrary"),
        name="out_b",
    )(u, u, h, x, w_in, conv_w, vec(conv_b), vec(ln_w), vec(ln_b), w, vec(nw))


def _retention_kernel(q_ref, k_ref, v_ref, g_ref, dm_ref, qd_ref, kd_ref, cd_ref, y_ref, state, *, tc, ch):
    @pl.when(pl.program_id(2) == 0)
    def _():
        state[...] = jnp.zeros(state.shape, F32)

    nt = (((1,), (1,)), ((), ()))
    tn = (((0,), (0,)), ((), ()))
    dm, qd, kd, cd = dm_ref[...], qd_ref[...], kd_ref[...], cd_ref[...]
    n = tc // ch
    rows = [slice(c * ch, (c + 1) * ch) for c in range(n)]
    qs = [q_ref[rs, :] for rs in rows]
    ks = [k_ref[rs, :] for rs in rows]
    vs = [v_ref[rs, :] for rs in rows]
    inner = [(lax.dot_general(q, k, nt, preferred_element_type=F32) * dm).astype(BF16) for q, k in zip(qs, ks)]
    kdec = [(k.astype(F32) * kd).astype(BF16) for k in ks]
    intra = [jnp.dot(a, v, preferred_element_type=F32) for a, v in zip(inner, vs)]
    update = [lax.dot_general(a, v, tn, preferred_element_type=F32) for a, v in zip(kdec, vs)]
    st = state[...]
    for c in range(n):
        cross = jnp.dot(qs[c], st.astype(BF16), preferred_element_type=F32)
        o = intra[c] + cross * qd
        st = st * cd + update[c]
        if c == n - 1:
            state[...] = st
        mu = jnp.mean(o, axis=-1, keepdims=True)
        cen = o - mu
        var = jnp.mean(cen * cen, axis=-1, keepdims=True)
        on = cen * lax.rsqrt(var + EPS)
        y_ref[rows[c], :] = (on * g_ref[rows[c], :].astype(F32)).astype(BF16)


def _retention(qk, v, gate, batch, seq, tc=TILES.retention, ch=TILES.retention_chunk):
    m = batch * seq
    nt = seq // tc
    vw = C_HEADS * C_V_DIM

    gammas = 1.0 - jnp.exp(jnp.linspace(math.log(1.0 / 32), math.log(1.0 / 512), C_HEADS, dtype=F32))
    log_g = jnp.log(gammas)
    idx = jnp.arange(ch, dtype=F32)
    diff = idx[:, None] - idx[None, :]
    decay_mask = jnp.where(diff >= 0, jnp.exp(jnp.maximum(diff, 0.0)[None] * log_g[:, None, None]), 0.0)
    q_decay = jnp.exp((idx[None] + 1.0) * log_g[:, None])
    k_decay = jnp.exp((ch - 1.0 - idx[None]) * log_g[:, None])
    chunk_decay = jnp.exp(ch * log_g)
    qd = jnp.broadcast_to(q_decay[:, :, None], (C_HEADS, ch, C_V_DIM))
    kd = jnp.broadcast_to(k_decay[:, :, None], (C_HEADS, ch, C_QK_DIM))
    cd = jnp.broadcast_to(chunk_decay[:, None, None], (C_HEADS, 1, C_V_DIM))

    head = lambda b, h, i: (h, 0, 0)
    rows = lambda off: (lambda b, h, i: (b * nt + i, off + h))
    return pl.pallas_call(
        functools.partial(_retention_kernel, tc=tc, ch=ch),
        out_shape=jax.ShapeDtypeStruct((m, vw), BF16),
        grid=(batch, C_HEADS, nt),
        in_specs=[pl.BlockSpec((tc, C_QK_DIM), rows(0)),
                  pl.BlockSpec((tc, C_QK_DIM), rows(C_HEADS)),
                  pl.BlockSpec((tc, C_V_DIM), rows(0)),
                  pl.BlockSpec((tc, C_V_DIM), rows(0)),
                  pl.BlockSpec((None, ch, ch), head),
                  pl.BlockSpec((None, ch, C_V_DIM), head),
                  pl.BlockSpec((None, ch, C_QK_DIM), head),
                  pl.BlockSpec((None, 1, C_V_DIM), head)],
        out_specs=pl.BlockSpec((tc, C_V_DIM), rows(0)),
        scratch_shapes=[pltpu.VMEM((C_QK_DIM, C_V_DIM), F32)],
        compiler_params=_params("parallel", "parallel", "arbitrary"),
        name="retention",
    )(qk, qk, v, gate, decay_mask, qd, kd, cd)


def _out_c_kernel(y_ref, x_ref, w_ref, nw_ref, *rest, last, dils):
    out_refs, slabs, _ = _split_tail(rest, 0, last, dils)
    y = jnp.dot(y_ref[...], w_ref[...], preferred_element_type=F32)
    _finish(y, x_ref, nw_ref, out_refs, slabs, last, dils)


def _out_c(y, x, w, nw, last, dils, tm=TILES.out_c):
    m, d = x.shape
    kdim = y.shape[1]
    row = lambda i: (i, 0)
    shapes, specs = _tail_outputs(m, d, tm, last, dils, lambda i: i)
    return pl.pallas_call(
        functools.partial(_out_c_kernel, last=last, dils=dils),
        out_shape=shapes,
        grid=(m // tm,),
        in_specs=[pl.BlockSpec((tm, kdim), row), pl.BlockSpec((tm, d), row),
                  _resident((kdim, d), lambda i: (0, 0)), _resident((1, d), lambda i: (0, 0))],
        out_specs=specs,
        scratch_shapes=[] if last else _slab_scratch(d, tm, dils),
        compiler_params=_params("parallel"),
        name="out_c",
    )(y, x, w, nw.reshape(1, d))


def kernel(x, positions, norm_w, final_norm_w, a_w_in, a_w_out, b_w_in, b_conv_w, b_conv_b, b_ln_w, b_ln_b,
           b_w_out, c_w_in, c_w_out):
    batch, seq, d = x.shape
    depth = norm_w.shape[0]
    m = batch * seq
    xs = x.reshape(m, d)
    perm_dils = tuple(dl for dl in A_DILATIONS if dl > 1)

    def next_dils(i):
        return perm_dils if (i + 1 < depth and (i + 1) % 3 == 0) else ()

    fa, fc = _rope_freqs()
    pos = positions.reshape(m, 1)
    tabs_a = _rope_a_tables(pos, fa, perm_dils)
    tabs_c = _rope_c_tables(pos, fc)

    hs = _rmsnorm(xs, norm_w[0], perm_dils)
    for i in range(depth):
        kind, j = i % 3, i // 3
        last = i == depth - 1
        nw = final_norm_w if last else norm_w[i + 1]
        dils = next_dils(i)
        if kind == 0:
            w_in = a_w_in[j].astype(BF16)
            os_, lses, proj0 = [], [], None
            for g, dl in enumerate(A_DILATIONS):
                h_g = hs[g].reshape(m, d)
                proj = _proj_a(h_g, w_in, tabs_a[g], g)
                proj0 = proj if g == 0 else proj0
                o, lse = _attention_group(proj, batch, seq, dl)
                os_.append(o)
                lses.append(lse)
            outs = _out_a(os_, lses, proj0, xs, a_w_out[j].astype(BF16), nw, last, dils)
        elif kind == 1:
            assert not dils
            w_in = b_w_in[j].astype(BF16)
            u = _proj_b_glu(hs[0], w_in)
            outs = _out_b(u, hs[0], xs, w_in, b_conv_w[j], b_conv_b[j], b_ln_w[j], b_ln_b[j],
                          b_w_out[j].astype(BF16), nw, seq, last)
        else:
            qk, v, gate = _proj_c(hs[0], c_w_in[j].astype(BF16), tabs_c)
            y = _retention(qk, v, gate, batch, seq)
            outs = _out_c(y, xs, c_w_out[j].astype(BF16), nw, last, dils)
        if last:
            return outs[0].reshape(batch, seq, d)
        xs, hs = outs[0], outs[1:]
```
